```python
import jax, jax.numpy as jnp
from jax import lax
import numpy as np

D_MODEL = 1024
BATCH = 16
SEQ = 2048
DEPTH = 4

GRID_W = 64
CTX_LEN = 256
N_MIXERS = 2
CONV_WIDTH = 3
GLA_HEADS = 4
GLA_DK_TOT = D_MODEL // 2
GLA_DV_TOT = D_MODEL
GLA_DK = GLA_DK_TOT // GLA_HEADS
GLA_DV = GLA_DV_TOT // GLA_HEADS
GLA_GATE_RANK = 16
GLA_TAU = 16.0
GLA_CHUNK = 64
D_FF = -(-8 * D_MODEL // (3 * 256)) * 256
N_ADA = 6
EPS = 1e-6

kernel_name = "hybrid_conv_gla_diffusion_trunk"


def rmsnorm(x, g):
    xf = x.astype(jnp.float32)
    y = xf * lax.rsqrt(jnp.mean(xf * xf, axis=-1, keepdims=True) + EPS)
    return (y * g.astype(jnp.float32)).astype(x.dtype)


def ada_terms(cond, w, b):
    return jnp.split(jax.nn.silu(cond) @ w + b, N_ADA, axis=-1)


def modulate(h, shift, scale):
    return h * (1.0 + scale) + shift


def conv3_seq(u, w):
    up = jnp.pad(u, ((0, 0), (1, 1), (0, 0)))
    return w[0] * up[:, :-2] + w[1] * up[:, 1:-1] + w[2] * up[:, 2:]


def conv3_grid(u, w):
    B, L, C = u.shape
    rows = L // GRID_W
    return conv3_seq(u.reshape(B * rows, GRID_W, C), w).reshape(B, L, C)


def short_conv_mixer(h, w_in, conv_w, w_out, on_grid):
    gate_b, gate_c, hv = jnp.split(h @ w_in, 3, axis=-1)
    u = gate_c * hv
    conv = conv3_grid(u, conv_w) if on_grid else conv3_seq(u, conv_w)
    return (gate_b * conv) @ w_out


def gla_project(h, w_in, b_r, w_a1, w_a2, b_a):
    B, L, _ = h.shape
    q, k, v, r = jnp.split(h @ w_in, [GLA_DK_TOT, 2 * GLA_DK_TOT, 2 * GLA_DK_TOT + GLA_DV_TOT], axis=-1)

    def heads(t, d):
        return t.reshape(B, L, GLA_HEADS, d).transpose(0, 2, 1, 3).astype(jnp.float32)

    q = heads(q, GLA_DK) * (GLA_DK ** -0.5)
    k = heads(k, GLA_DK)
    v = heads(v, GLA_DV)
    logit = jnp.einsum('zblr,zrk->zblk', jnp.einsum('bld,zdr->zblr', h, w_a1), w_a2) + b_a[:, None, None, :]
    la = jax.nn.log_sigmoid(logit.astype(jnp.float32)) / GLA_TAU
    la = la.reshape(2, B, L, GLA_HEADS, GLA_DK).transpose(0, 1, 3, 2, 4)
    return q, k, v, r + b_r, la


def gla_chunked(q, k, v, la, s0, with_output):
    B, H, T, dk = q.shape
    dv = v.shape[-1]
    n = T // GLA_CHUNK
    q = q.reshape(B, H, n, GLA_CHUNK, dk)
    k = k.reshape(B, H, n, GLA_CHUNK, dk)
    v = v.reshape(B, H, n, GLA_CHUNK, dv)
    b = jnp.cumsum(la.reshape(B, H, n, GLA_CHUNK, dk), axis=3)
    b_last = b[:, :, :, -1]
    k_dec = k * jnp.exp(b_last[:, :, :, None] - b)
    u = jnp.einsum('bhnsd,bhnse->bhnde', k_dec, v)

    def step(state, inp):
        decay, u_n = inp
        return decay[..., None] * state + u_n, state

    s_final, s_prev = lax.scan(step, s0, (jnp.moveaxis(jnp.exp(b_last), 2, 0), jnp.moveaxis(u, 2, 0)))
    if not with_output:
        return None, s_final
    s_prev = jnp.moveaxis(s_prev, 0, 2)
    q_dec = q * jnp.exp(b)
    o_inter = jnp.einsum('bhncd,bhnde->bhnce', q_dec, s_prev)
    scores = jnp.einsum('bhncd,bhnsd->bhncs', q_dec, k * jnp.exp(-b))
    mask = jnp.tril(jnp.ones((GLA_CHUNK, GLA_CHUNK), dtype=bool))
    scores = jnp.where(mask, scores, 0.0)
    o_intra = jnp.einsum('bhncs,bhnse->bhnce', scores, v)
    return (o_inter + o_intra).reshape(B, H, T, dv), s_final


def gla_output(o, r, norm_g, w_out, dtype):
    B, H, T, dv = o.shape
    o = rmsnorm(o.transpose(0, 2, 1, 3), norm_g).reshape(B, T, H * dv).astype(dtype)
    return (o * jax.nn.silu(r)) @ w_out


def gla_mixer(hl, hc, w_in, b_r, w_a1, w_a2, b_a, norm_g, w_out, ctx_out):
    ql, kl, vl, rl, la_l = gla_project(hl, w_in, b_r, w_a1, w_a2, b_a)
    qc, kc, vc, rc, la_c = gla_project(hc, w_in, b_r, w_a1, w_a2, b_a)
    flip = lambda t: jnp.flip(t, axis=2)
    s0 = jnp.zeros((hl.shape[0], GLA_HEADS, GLA_DK, GLA_DV), jnp.float32)
    o_cf, s_cf = gla_chunked(qc, kc, vc, la_c[0], s0, ctx_out)
    o_cb, s_cb = gla_chunked(flip(qc), flip(kc), flip(vc), flip(la_c[1]), s0, ctx_out)
    o_lf, _ = gla_chunked(ql, kl, vl, la_l[0], s_cf, True)
    o_lb, _ = gla_chunked(flip(ql), flip(kl), flip(vl), flip(la_l[1]), s_cb, True)
    yl = gla_output(o_lf + flip(o_lb), rl, norm_g, w_out, hl.dtype)
    yc = gla_output(o_cf + flip(o_cb), rc, norm_g, w_out, hc.dtype) if ctx_out else None
    return yl, yc


def swiglu(h, w_in, w_out):
    gate, up = jnp.split(h @ w_in, 2, axis=-1)
    return (jax.nn.silu(gate) * up) @ w_out


def setup_inputs(seed: int = 0) -> dict:
    key = jax.random.key(seed)
    ks = jax.random.split(key, 24)
    n_conv = (DEPTH + N_MIXERS - 1) // N_MIXERS
    n_gla = DEPTH // N_MIXERS
    nrm = lambda k, shape, s: jax.random.normal(k, shape, jnp.float32) * s
    d = D_MODEL
    return {
        "x": nrm(ks[0], (BATCH, SEQ, d), 1.0),
        "c": nrm(ks[1], (BATCH, d), 1.0),
        "ctx": nrm(ks[2], (BATCH, CTX_LEN, d), 1.0),
        "c_ctx": nrm(ks[3], (d,), 1.0),
        "ada_w": nrm(ks[4], (DEPTH, d, N_ADA * d), 0.5 * d ** -0.5),
        "ada_b": nrm(ks[5], (DEPTH, N_ADA * d), 0.01),
        "norm1_g": 1.0 + nrm(ks[6], (DEPTH, d), 0.02),
        "norm2_g": 1.0 + nrm(ks[7], (DEPTH, d), 0.02),
        "conv_w_in": nrm(ks[8], (n_conv, d, 3 * d), d ** -0.5),
        "conv_w": nrm(ks[9], (n_conv, CONV_WIDTH, d), CONV_WIDTH ** -0.5),
        "conv_w_out": nrm(ks[10], (n_conv, d, d), d ** -0.5),
        "gla_w_in": nrm(ks[11], (n_gla, d, 2 * GLA_DK_TOT + 2 * GLA_DV_TOT), d ** -0.5),
        "gla_b_r": nrm(ks[12], (n_gla, GLA_DV_TOT), 0.01),
        "gla_w_a1": nrm(ks[13], (n_gla, 2, d, GLA_GATE_RANK), d ** -0.5),
        "gla_w_a2": nrm(ks[14], (n_gla, 2, GLA_GATE_RANK, GLA_DK_TOT), GLA_GATE_RANK ** -0.5),
        "gla_b_a": nrm(ks[15], (n_gla, 2, GLA_DK_TOT), 0.01),
        "gla_norm_g": 1.0 + nrm(ks[16], (n_gla, GLA_DV), 0.02),
        "gla_w_out": nrm(ks[17], (n_gla, GLA_DV_TOT, d), GLA_DV_TOT ** -0.5),
        "ffn_w_in": nrm(ks[18], (DEPTH, d, 2 * D_FF), d ** -0.5),
        "ffn_w_out": nrm(ks[19], (DEPTH, D_FF, d), D_FF ** -0.5),
        "final_g": 1.0 + nrm(ks[20], (d,), 0.02),
    }


def reference(x, c, ctx, c_ctx, ada_w, ada_b, norm1_g, norm2_g, conv_w_in, conv_w, conv_w_out,
              gla_w_in, gla_b_r, gla_w_a1, gla_w_a2, gla_b_a, gla_norm_g, gla_w_out,
              ffn_w_in, ffn_w_out, final_g):
    cond_lat = c[:, None, :]
    for i in range(DEPTH):
        last = i == DEPTH - 1
        kind = i % N_MIXERS
        j = i // N_MIXERS
        need_ctx_in = (not last) or kind == 1
        sh1, sc1, g1, sh2, sc2, g2 = ada_terms(cond_lat, ada_w[i], ada_b[i])
        hl = modulate(rmsnorm(x, norm1_g[i]), sh1, sc1)
        if need_ctx_in:
            csh1, csc1, cg1, csh2, csc2, cg2 = ada_terms(c_ctx, ada_w[i], ada_b[i])
            hc = modulate(rmsnorm(ctx, norm1_g[i]), csh1, csc1)
        if kind == 0:
            yl = short_conv_mixer(hl, conv_w_in[j], conv_w[j], conv_w_out[j], True)
            yc = short_conv_mixer(hc, conv_w_in[j], conv_w[j], conv_w_out[j], False) if not last else None
        else:
            yl, yc = gla_mixer(hl, hc, gla_w_in[j], gla_b_r[j], gla_w_a1[j], gla_w_a2[j], gla_b_a[j],
                               gla_norm_g[j], gla_w_out[j], not last)
        x = x + g1 * yl
        x = x + g2 * swiglu(modulate(rmsnorm(x, norm2_g[i]), sh2, sc2), ffn_w_in[i], ffn_w_out[i])
        if not last:
            ctx = ctx + cg1 * yc
            ctx = ctx + cg2 * swiglu(modulate(rmsnorm(ctx, norm2_g[i]), csh2, csc2), ffn_w_in[i], ffn_w_out[i])
    return rmsnorm(x, final_g)
```

```python
import functools

import jax
import jax.numpy as jnp
from jax import lax
from jax.experimental import pallas as pl
from jax.experimental.pallas import tpu as pltpu

F32 = jnp.float32
BF16 = jnp.bfloat16

D_MODEL = 1024
DEPTH = 4
GRID_W = 64
N_ADA = 6
GLA_HEADS = 4
GLA_DK = 128
GLA_DV = 256
GLA_DK_TOT = GLA_HEADS * GLA_DK
GLA_DV_TOT = GLA_HEADS * GLA_DV
GLA_GATE_RANK = 16
GLA_TAU = 16.0
GLA_CHUNK = 64
EPS = 1e-6

TOKEN_TILE = 512
CUMSUM_ROWS = 256
ADA_ROWS = 24
CTX_ADA_ROW = 16
ADA_COL_TILE = 1536
GATE_PAD = 128
V7X_VMEM_LIMIT = 56 * 1024 * 1024


def _rms(x, g):
    ms = jnp.mean(x * x, axis=-1, keepdims=True)
    return x * lax.rsqrt(ms + EPS) * g


def _silu(x):
    return x * jax.nn.sigmoid(x)


def _log_sigmoid(x):
    return jnp.minimum(x, 0.0) - jnp.log1p(jnp.exp(-jnp.abs(x)))


def _dot(a, b):
    return jnp.dot(a, b, preferred_element_type=F32)


def _resident(shape):
    nd = len(shape)
    return pl.BlockSpec(shape, lambda *_: (0,) * nd, pipeline_mode=pl.Buffered(1))


def _params(n_axes):
    return pltpu.CompilerParams(
        dimension_semantics=("arbitrary",) * n_axes,
        vmem_limit_bytes=V7X_VMEM_LIMIT)


def _ada_kernel(cond_ref, w_ref, b_ref, o_ref):
    s = _silu(cond_ref[...]).astype(BF16)
    o_ref[...] = _dot(s, w_ref[...].astype(BF16)) + b_ref[...]


def _ada_call(cond, ada_w, ada_b):
    depth, d, n = ada_w.shape
    return pl.pallas_call(
        _ada_kernel,
        grid=(depth, n // ADA_COL_TILE),
        in_specs=[
            pl.BlockSpec((ADA_ROWS, d), lambda i, j: (0, 0)),
            pl.BlockSpec((None, d, ADA_COL_TILE), lambda i, j: (i, 0, j)),
            pl.BlockSpec((None, 1, ADA_COL_TILE), lambda i, j: (i, 0, j)),
        ],
        out_specs=pl.BlockSpec((None, ADA_ROWS, ADA_COL_TILE), lambda i, j: (i, 0, j)),
        out_shape=jax.ShapeDtypeStruct((depth, ADA_ROWS, n), F32),
        compiler_params=_params(2),
        name="ada_terms",
    )(cond, ada_w, ada_b.reshape(depth, 1, n))


def _ada_slices(ada, first):
    d = D_MODEL
    return tuple(ada[:, (first + t) * d:(first + t + 1) * d] for t in range(3))


def _ada_spec(rows_per_cond):
    if rows_per_cond is None:
        return pl.BlockSpec((None, 1, N_ADA * D_MODEL), lambda t: (CTX_ADA_ROW, 0, 0))
    tiles = rows_per_cond // TOKEN_TILE
    return pl.BlockSpec((None, 1, N_ADA * D_MODEL), lambda t: (t // tiles, 0, 0))


def _tile_spec(width):
    return pl.BlockSpec((TOKEN_TILE, width), lambda t: (t, 0))


def _conv_mix_kernel(x_ref, ada_ref, g_ref, win_ref, cw_ref, wout_ref, o_ref, *, seg):
    d = D_MODEL
    x = x_ref[...]
    sh, sc, gt = _ada_slices(ada_ref[...], 0)
    h = _rms(x, g_ref[...]) * (1.0 + sc) + sh
    p = _dot(h.astype(BF16), win_ref[...])
    gate_b = p[:, :d]
    u = p[:, d:2 * d] * p[:, 2 * d:]
    tm = x.shape[0]
    pos = lax.broadcasted_iota(jnp.int32, (tm, d), 0) & (seg - 1)
    u_prev = jnp.where(pos == 0, 0.0, pltpu.roll(u, 1, 0))
    u_next = jnp.where(pos == seg - 1, 0.0, pltpu.roll(u, tm - 1, 0))
    cw = cw_ref[...]
    conv = cw[0:1] * u_prev + cw[1:2] * u + cw[2:3] * u_next
    y = _dot((gate_b * conv).astype(BF16), wout_ref[...])
    o_ref[...] = x + gt * y


def _conv_mix_call(x, ada3, g, w_in, cw, w_out, *, seg, rows_per_cond):
    n, d = x.shape
    return pl.pallas_call(
        functools.partial(_conv_mix_kernel, seg=seg),
        grid=(n // TOKEN_TILE,),
        in_specs=[
            _tile_spec(d), _ada_spec(rows_per_cond), _resident((1, d)),
            _resident(w_in.shape), _resident(cw.shape), _resident(w_out.shape),
        ],
        out_specs=_tile_spec(d),
        out_shape=jax.ShapeDtypeStruct((n, d), F32),
        compiler_params=_params(1),
        name="conv_mixer",
    )(x, ada3, g, w_in, cw, w_out)


def _ffn_kernel(x_ref, ada_ref, g_ref, win_ref, wout_ref, fg_ref, o_ref, *, final):
    x = x_ref[...]
    sh, sc, gt = _ada_slices(ada_ref[...], 3)
    h = (_rms(x, g_ref[...]) * (1.0 + sc) + sh).astype(BF16)
    gu = _dot(h, win_ref[...])
    f = gu.shape[1] // 2
    a = _silu(gu[:, :f]) * gu[:, f:]
    y = _dot(a.astype(BF16), wout_ref[...])
    out = x + gt * y
    if final:
        out = _rms(out, fg_ref[...])
    o_ref[...] = out


def _ffn_call(x, ada3, g, w_in, w_out, final_g, *, rows_per_cond, final):
    n, d = x.shape
    return pl.pallas_call(
        functools.partial(_ffn_kernel, final=final),
        grid=(n // TOKEN_TILE,),
        in_specs=[
            _tile_spec(d), _ada_spec(rows_per_cond), _resident((1, d)),
            _resident(w_in.shape), _resident(w_out.shape), _resident((1, d)),
        ],
        out_specs=_tile_spec(d),
        out_shape=jax.ShapeDtypeStruct((n, d), F32),
        compiler_params=_params(1),
        name="swiglu_ffn",
    )(x, ada3, g, w_in, w_out, final_g)


def _chunk_tri(upper):
    r = lax.broadcasted_iota(jnp.int32, (CUMSUM_ROWS, CUMSUM_ROWS), 0)
    c = lax.broadcasted_iota(jnp.int32, (CUMSUM_ROWS, CUMSUM_ROWS), 1)
    same = (r // GLA_CHUNK) == (c // GLA_CHUNK)
    keep = (c >= r) if upper else (c <= r)
    return jnp.where(same & keep, 1.0, 0.0).astype(BF16)


def _chunk_cumsum(tri, hi, lo):
    parts = []
    for j in range(hi.shape[0] // CUMSUM_ROWS):
        rows = slice(j * CUMSUM_ROWS, (j + 1) * CUMSUM_ROWS)
        parts.append(_dot(tri, hi[rows]) + _dot(tri, lo[rows]))
    return jnp.concatenate(parts, axis=0)


def _gla_proj_kernel(x_ref, ada_ref, g_ref, wcat_ref, br_ref, wa2_ref, ba_ref,
                     qdf_ref, kif_ref, kdf_ref, qdb_ref, kib_ref, kdb_ref,
                     v_ref, rs_ref, dlf_ref, dlb_ref):
    kt, vt = GLA_DK_TOT, GLA_DV_TOT
    x = x_ref[...]
    tm = x.shape[0]
    nch = tm // GLA_CHUNK
    sh, sc, _ = _ada_slices(ada_ref[...], 0)
    h = (_rms(x, g_ref[...]) * (1.0 + sc) + sh).astype(BF16)
    p = _dot(h, wcat_ref[...])
    q = p[:, :kt] * (GLA_DK ** -0.5)
    k = p[:, kt:2 * kt]
    v_ref[...] = p[:, 2 * kt:2 * kt + vt].astype(BF16)
    rs_ref[...] = _silu(p[:, 2 * kt + vt:2 * kt + 2 * vt] + br_ref[...])
    a1 = p[:, 2 * kt + 2 * vt:].astype(BF16)
    logit = _dot(a1, wa2_ref[...]) + ba_ref[...]
    la = _log_sigmoid(logit) * (1.0 / GLA_TAU)
    la_hi = la.astype(BF16)
    la_lo = (la - la_hi.astype(F32)).astype(BF16)

    q3 = q.reshape(nch, GLA_CHUNK, kt)
    k3 = k.reshape(nch, GLA_CHUNK, kt)
    for upper, qd_ref, ki_ref, kd_ref, dl_ref in (
            (False, qdf_ref, kif_ref, kdf_ref, dlf_ref),
            (True, qdb_ref, kib_ref, kdb_ref, dlb_ref)):
        cols = slice(kt, 2 * kt) if upper else slice(0, kt)
        b = _chunk_cumsum(_chunk_tri(upper), la_hi[:, cols], la_lo[:, cols])
        b3 = b.reshape(nch, GLA_CHUNK, kt)
        last = 0 if upper else GLA_CHUNK - 1
        b_last = b3[:, last:last + 1, :]
        qd_ref[...] = (q3 * jnp.exp(b3)).reshape(tm, kt).astype(BF16)
        ki_ref[...] = (k3 * jnp.exp(-b3)).reshape(tm, kt).astype(BF16)
        kd_ref[...] = (k3 * jnp.exp(b_last - b3)).reshape(tm, kt).astype(BF16)
        dl_ref[...] = jnp.exp(jnp.sum(la[:, cols].reshape(nch, GLA_CHUNK, kt), axis=1))


def _gla_proj_call(x, ada3, g, wcat, br, wa2, ba, *, rows_per_cond):
    n, d = x.shape
    kt, vt = GLA_DK_TOT, GLA_DV_TOT
    nch = TOKEN_TILE // GLA_CHUNK
    kspec = _tile_spec(kt)
    dl_spec = pl.BlockSpec((nch, kt), lambda t: (t, 0))
    kshape = jax.ShapeDtypeStruct((n, kt), BF16)
    dl_shape = jax.ShapeDtypeStruct((n // GLA_CHUNK, kt), F32)
    return pl.pallas_call(
        _gla_proj_kernel,
        grid=(n // TOKEN_TILE,),
        in_specs=[
            _tile_spec(d), _ada_spec(rows_per_cond), _resident((1, d)),
            _resident(wcat.shape), _resident(br.shape), _resident(wa2.shape),
            _resident(ba.shape),
        ],
        out_specs=[kspec] * 6 + [_tile_spec(vt), _tile_spec(vt), dl_spec, dl_spec],
        out_shape=[kshape] * 6 + [
            jax.ShapeDtypeStruct((n, vt), BF16), jax.ShapeDtypeStruct((n, vt), F32),
            dl_shape, dl_shape],
        compiler_params=_params(1),
        name="gla_proj",
    )(x, ada3, g, wcat, br, wa2, ba)


def _gla_scan_kernel(*refs, ctx_out):
    lat = refs[0:9]
    ctx = refs[9:18]
    o_l = refs[18]
    o_c = refs[19] if ctx_out else None
    s_f, s_b = refs[-2], refs[-1]

    ch = GLA_CHUNK
    r = lax.broadcasted_iota(jnp.int32, (ch, ch), 0)
    c = lax.broadcasted_iota(jnp.int32, (ch, ch), 1)
    mask_f = c <= r
    mask_b = c >= r
    er = lax.broadcasted_iota(jnp.int32, (GLA_DK, GLA_DK), 0)
    ec = lax.broadcasted_iota(jnp.int32, (GLA_DK, GLA_DK), 1)
    eye = er == ec

    def step(qd_ref, ki_ref, kd_ref, v_ref, dl_ref, s_ref, idx, mask, o_ref):
        rows = pl.ds(pl.multiple_of(idx * ch, ch), ch)
        qd = qd_ref[rows, :]
        ki = ki_ref[rows, :]
        kd = kd_ref[rows, :]
        v = v_ref[rows, :]
        d_row = dl_ref[pl.ds(idx, 1), :]
        s = s_ref[...]
        if o_ref is not None:
            scores = lax.dot_general(qd, ki, (((1,), (1,)), ((), ())),
                                     preferred_element_type=F32)
            pm = jnp.where(mask, scores, 0.0).astype(BF16)
            o_ref[rows, :] += _dot(qd, s.astype(BF16)) + _dot(pm, v)
        u = lax.dot_general(kd, v, (((0,), (0,)), ((), ())),
                            preferred_element_type=F32)
        d_col = jnp.sum(jnp.where(eye, d_row, 0.0), axis=1, keepdims=True)
        s_ref[...] = d_col * s + u

    def both(group, o_ref, i, n):
        qdf, kif, kdf, qdb, kib, kdb, v, dlf, dlb = group
        step(qdf, kif, kdf, v, dlf, s_f, i, mask_f, o_ref)
        step(qdb, kib, kdb, v, dlb, s_b, n - 1 - i, mask_b, o_ref)

    s_f[...] = jnp.zeros_like(s_f)
    s_b[...] = jnp.zeros_like(s_b)
    o_l[...] = jnp.zeros_like(o_l)
    if ctx_out:
        o_c[...] = jnp.zeros_like(o_c)

    n_c = ctx[0].shape[0] // ch
    n_l = lat[0].shape[0] // ch
    for i in range(n_c):
        both(ctx, o_c, i, n_c)

    def body(i, carry):
        both(lat, o_l, i, n_l)
        return carry

    lax.fori_loop(0, n_l, body, 0)


def _gla_scan_call(lat, ctx, *, batch, ctx_out):
    def specs(group):
        t = group[0].shape[1]
        nch = t // GLA_CHUNK
        kspec = pl.BlockSpec((None, t, GLA_DK), lambda b, h: (b, 0, h))
        vspec = pl.BlockSpec((None, t, GLA_DV), lambda b, h: (b, 0, h))
        dspec = pl.BlockSpec((None, nch, GLA_DK), lambda b, h: (b, 0, h))
        return [kspec] * 6 + [vspec, dspec, dspec]

    t_l, t_c = lat[0].shape[1], ctx[0].shape[1]
    out_specs = [pl.BlockSpec((None, t_l, GLA_DV), lambda b, h: (b, 0, h))]
    out_shape = [jax.ShapeDtypeStruct((batch, t_l, GLA_DV_TOT), F32)]
    if ctx_out:
        out_specs.append(pl.BlockSpec((None, t_c, GLA_DV), lambda b, h: (b, 0, h)))
        out_shape.append(jax.ShapeDtypeStruct((batch, t_c, GLA_DV_TOT), F32))
    return pl.pallas_call(
        functools.partial(_gla_scan_kernel, ctx_out=ctx_out),
        grid=(batch, GLA_HEADS),
        in_specs=specs(lat) + specs(ctx),
        out_specs=out_specs,
        out_shape=out_shape,
        scratch_shapes=[pltpu.VMEM((GLA_DK, GLA_DV), F32),
                        pltpu.VMEM((GLA_DK, GLA_DV), F32)],
        compiler_params=_params(2),
        name="gla_scan",
    )(*lat, *ctx)


def _gla_out_kernel(x_ref, o_ref, rs_ref, ada_ref, ng_ref, wout_ref, out_ref):
    x = x_ref[...]
    o = o_ref[...]
    _, _, gt = _ada_slices(ada_ref[...], 0)
    ng = ng_ref[...]
    heads = [_rms(o[:, h * GLA_DV:(h + 1) * GLA_DV], ng) for h in range(GLA_HEADS)]
    on = jnp.concatenate(heads, axis=1)
    y = _dot((on * rs_ref[...]).astype(BF16), wout_ref[...])
    out_ref[...] = x + gt * y


def _gla_out_call(x, o, rs, ada3, ng, w_out, *, rows_per_cond):
    n, d = x.shape
    return pl.pallas_call(
        _gla_out_kernel,
        grid=(n // TOKEN_TILE,),
        in_specs=[
            _tile_spec(d), _tile_spec(GLA_DV_TOT), _tile_spec(GLA_DV_TOT),
            _ada_spec(rows_per_cond), _resident(ng.shape), _resident(w_out.shape),
        ],
        out_specs=_tile_spec(d),
        out_shape=jax.ShapeDtypeStruct((n, d), F32),
        compiler_params=_params(1),
        name="gla_out",
    )(x, o, rs, ada3, ng, w_out)


def _gla_weights(w_in, w_a1, w_a2, b_a):
    d = w_in.shape[0]
    pad = jnp.zeros((d, GATE_PAD - 2 * GLA_GATE_RANK), w_in.dtype)
    wcat = jnp.concatenate([w_in, w_a1[0], w_a1[1], pad], axis=1).astype(BF16)
    wa2 = jnp.zeros((GATE_PAD, 2 * GLA_DK_TOT), w_a2.dtype)
    wa2 = wa2.at[:GLA_GATE_RANK, :GLA_DK_TOT].set(w_a2[0])
    wa2 = wa2.at[GLA_GATE_RANK:2 * GLA_GATE_RANK, GLA_DK_TOT:].set(w_a2[1])
    return wcat, wa2.astype(BF16), b_a.reshape(1, 2 * GLA_DK_TOT)


def kernel(x, c, ctx, c_ctx, ada_w, ada_b, norm1_g, norm2_g, conv_w_in, conv_w, conv_w_out,
           gla_w_in, gla_b_r, gla_w_a1, gla_w_a2, gla_b_a, gla_norm_g, gla_w_out,
           ffn_w_in, ffn_w_out, final_g):
    batch, seq, d = x.shape
    ctx_len = ctx.shape[1]
    assert d == D_MODEL and batch == CTX_ADA_ROW < ADA_ROWS
    assert seq % TOKEN_TILE == 0 and (batch * ctx_len) % TOKEN_TILE == 0
    assert TOKEN_TILE % ctx_len == 0 and TOKEN_TILE % GRID_W == 0

    xl = x.reshape(batch * seq, d)
    xc = ctx.reshape(batch * ctx_len, d)
    cond = jnp.concatenate(
        [c, c_ctx[None, :], jnp.zeros((ADA_ROWS - batch - 1, d), c.dtype)], axis=0)
    ada_all = _ada_call(cond, ada_w, ada_b)
    fg = final_g.reshape(1, d)

    for i in range(DEPTH):
        last = i == DEPTH - 1
        kind, j = i % 2, i // 2
        ada3 = ada_all[i].reshape(ADA_ROWS, 1, N_ADA * d)
        g1 = norm1_g[i].reshape(1, d)
        g2 = norm2_g[i].reshape(1, d)
        if kind == 0:
            w_in = conv_w_in[j].astype(BF16)
            w_out = conv_w_out[j].astype(BF16)
            xl = _conv_mix_call(xl, ada3, g1, w_in, conv_w[j], w_out,
                                seg=GRID_W, rows_per_cond=seq)
            if not last:
                xc = _conv_mix_call(xc, ada3, g1, w_in, conv_w[j], w_out,
                                    seg=ctx_len, rows_per_cond=None)
        else:
            wcat, wa2, ba = _gla_weights(gla_w_in[j], gla_w_a1[j], gla_w_a2[j], gla_b_a[j])
            br = gla_b_r[j].reshape(1, GLA_DV_TOT)
            pl_out = _gla_proj_call(xl, ada3, g1, wcat, br, wa2, ba, rows_per_cond=seq)
            pc_out = _gla_proj_call(xc, ada3, g1, wcat, br, wa2, ba, rows_per_cond=None)

            def per_batch(arrs, t):
                return [a.reshape(batch, -1, a.shape[-1]) for a in arrs]

            lat = per_batch(pl_out[:7] + pl_out[8:], seq)
            cx = per_batch(pc_out[:7] + pc_out[8:], ctx_len)
            outs = _gla_scan_call(lat, cx, batch=batch, ctx_out=not last)
            ng = gla_norm_g[j].reshape(1, GLA_DV)
            w_out = gla_w_out[j].astype(BF16)
            xl = _gla_out_call(xl, outs[0].reshape(batch * seq, GLA_DV_TOT), pl_out[7],
                               ada3, ng, w_out, rows_per_cond=seq)
            if not last:
                xc = _gla_out_call(xc, outs[1].reshape(batch * ctx_len, GLA_DV_TOT),
                                   pc_out[7], ada3, ng, w_out, rows_per_cond=None)
        f_in = ffn_w_in[i].astype(BF16)
        f_out = ffn_w_out[i].astype(BF16)
        xl = _ffn_call(xl, ada3, g2, f_in, f_out, fg, rows_per_cond=seq, final=last)
        if not last:
            xc = _ffn_call(xc, ada3, g2, f_in, f_out, fg, rows_per_cond=None, final=False)
    return xl.reshape(batch, seq, d)
```

```python
import functools

import jax
import jax.numpy as jnp
from jax import lax
from jax.experimental import pallas as pl
from jax.experimental.pallas import tpu as pltpu

F32 = jnp.float32
BF16 = jnp.bfloat16

D_MODEL = 1024
DEPTH = 4
GRID_W = 64
N_ADA = 6
GLA_HEADS = 4
GLA_DK = 128
GLA_DV = 256
GLA_DK_TOT = GLA_HEADS * GLA_DK
GLA_DV_TOT = GLA_HEADS * GLA_DV
GLA_GATE_RANK = 16
GLA_TAU = 16.0
GLA_CHUNK = 64
EPS = 1e-6

TOKEN_TILE = 512
CUMSUM_ROWS = 256
ADA_ROWS = 24
CTX_ADA_ROW = 16
ADA_COL_TILE = 1536
GATE_PAD = 128
DECAY_ROW_PAD = 8
V7X_VMEM_LIMIT = 56 * 1024 * 1024


def _rms(x, g):
    ms = jnp.mean(x * x, axis=-1, keepdims=True)
    return x * lax.rsqrt(ms + EPS) * g


def _silu(x):
    return x * jax.nn.sigmoid(x)


def _log_sigmoid(x):
    return jnp.minimum(x, 0.0) - jnp.log1p(jnp.exp(-jnp.abs(x)))


def _dot(a, b):
    return jnp.dot(a, b, preferred_element_type=F32)


def _resident(shape):
    nd = len(shape)
    return pl.BlockSpec(shape, lambda *_: (0,) * nd, pipeline_mode=pl.Buffered(1))


def _params(n_axes):
    return pltpu.CompilerParams(
        dimension_semantics=("arbitrary",) * n_axes,
        vmem_limit_bytes=V7X_VMEM_LIMIT)


def _ada_kernel(cond_ref, w_ref, b_ref, o_ref):
    s = _silu(cond_ref[...]).astype(BF16)
    o_ref[...] = _dot(s, w_ref[...].astype(BF16)) + b_ref[...]


def _ada_call(cond, ada_w, ada_b):
    depth, d, n = ada_w.shape
    return pl.pallas_call(
        _ada_kernel,
        grid=(depth, n // ADA_COL_TILE),
        in_specs=[
            pl.BlockSpec((ADA_ROWS, d), lambda i, j: (0, 0)),
            pl.BlockSpec((None, d, ADA_COL_TILE), lambda i, j: (i, 0, j)),
            pl.BlockSpec((None, 1, ADA_COL_TILE), lambda i, j: (i, 0, j)),
        ],
        out_specs=pl.BlockSpec((None, ADA_ROWS, ADA_COL_TILE), lambda i, j: (i, 0, j)),
        out_shape=jax.ShapeDtypeStruct((depth, ADA_ROWS, n), F32),
        compiler_params=_params(2),
        name="ada_terms",
    )(cond, ada_w, ada_b.reshape(depth, 1, n))


def _ada_slices(ada, first):
    d = D_MODEL
    return tuple(ada[:, (first + t) * d:(first + t + 1) * d] for t in range(3))


def _ada_spec(rows_per_cond):
    if rows_per_cond is None:
        return pl.BlockSpec((None, 1, N_ADA * D_MODEL), lambda t: (CTX_ADA_ROW, 0, 0))
    tiles = rows_per_cond // TOKEN_TILE
    return pl.BlockSpec((None, 1, N_ADA * D_MODEL), lambda t: (t // tiles, 0, 0))


def _tile_spec(width):
    return pl.BlockSpec((TOKEN_TILE, width), lambda t: (t, 0))


def _conv_mix_kernel(x_ref, ada_ref, g_ref, win_ref, cw_ref, wout_ref, o_ref, *, seg):
    d = D_MODEL
    x = x_ref[...]
    sh, sc, gt = _ada_slices(ada_ref[...], 0)
    h = _rms(x, g_ref[...]) * (1.0 + sc) + sh
    p = _dot(h.astype(BF16), win_ref[...])
    gate_b = p[:, :d]
    u = p[:, d:2 * d] * p[:, 2 * d:]
    tm = x.shape[0]
    pos = lax.broadcasted_iota(jnp.int32, (tm, d), 0) & (seg - 1)
    u_prev = jnp.where(pos == 0, 0.0, pltpu.roll(u, 1, 0))
    u_next = jnp.where(pos == seg - 1, 0.0, pltpu.roll(u, tm - 1, 0))
    cw = cw_ref[...]
    conv = cw[0:1] * u_prev + cw[1:2] * u + cw[2:3] * u_next
    y = _dot((gate_b * conv).astype(BF16), wout_ref[...])
    o_ref[...] = x + gt * y


def _conv_mix_call(x, ada3, g, w_in, cw, w_out, *, seg, rows_per_cond):
    n, d = x.shape
    return pl.pallas_call(
        functools.partial(_conv_mix_kernel, seg=seg),
        grid=(n // TOKEN_TILE,),
        in_specs=[
            _tile_spec(d), _ada_spec(rows_per_cond), _resident((1, d)),
            _resident(w_in.shape), _resident(cw.shape), _resident(w_out.shape),
        ],
        out_specs=_tile_spec(d),
        out_shape=jax.ShapeDtypeStruct((n, d), F32),
        compiler_params=_params(1),
        name="conv_mixer",
    )(x, ada3, g, w_in, cw, w_out)


def _ffn_kernel(x_ref, ada_ref, g_ref, win_ref, wout_ref, fg_ref, o_ref, *, final):
    x = x_ref[...]
    sh, sc, gt = _ada_slices(ada_ref[...], 3)
    h = (_rms(x, g_ref[...]) * (1.0 + sc) + sh).astype(BF16)
    gu = _dot(h, win_ref[...])
    f = gu.shape[1] // 2
    a = _silu(gu[:, :f]) * gu[:, f:]
    y = _dot(a.astype(BF16), wout_ref[...])
    out = x + gt * y
    if final:
        out = _rms(out, fg_ref[...])
    o_ref[...] = out


def _ffn_call(x, ada3, g, w_in, w_out, final_g, *, rows_per_cond, final):
    n, d = x.shape
    return pl.pallas_call(
        functools.partial(_ffn_kernel, final=final),
        grid=(n // TOKEN_TILE,),
        in_specs=[
            _tile_spec(d), _ada_spec(rows_per_cond), _resident((1, d)),
            _resident(w_in.shape), _resident(w_out.shape), _resident((1, d)),
        ],
        out_specs=_tile_spec(d),
        out_shape=jax.ShapeDtypeStruct((n, d), F32),
        compiler_params=_params(1),
        name="swiglu_ffn",
    )(x, ada3, g, w_in, w_out, final_g)


def _chunk_tri(upper):
    r = lax.broadcasted_iota(jnp.int32, (CUMSUM_ROWS, CUMSUM_ROWS), 0)
    c = lax.broadcasted_iota(jnp.int32, (CUMSUM_ROWS, CUMSUM_ROWS), 1)
    same = (r // GLA_CHUNK) == (c // GLA_CHUNK)
    keep = (c >= r) if upper else (c <= r)
    return jnp.where(same & keep, 1.0, 0.0).astype(BF16)


def _chunk_cumsum(tri, hi, lo):
    parts = []
    for j in range(hi.shape[0] // CUMSUM_ROWS):
        rows = slice(j * CUMSUM_ROWS, (j + 1) * CUMSUM_ROWS)
        parts.append(_dot(tri, hi[rows]) + _dot(tri, lo[rows]))
    return jnp.concatenate(parts, axis=0)


def _gla_proj_kernel(x_ref, ada_ref, g_ref, wcat_ref, br_ref, wa2_ref, ba_ref,
                     qdf_ref, kif_ref, kdf_ref, qdb_ref, kib_ref, kdb_ref,
                     v_ref, rs_ref, dlf_ref, dlb_ref):
    kt, vt = GLA_DK_TOT, GLA_DV_TOT
    x = x_ref[...]
    tm = x.shape[0]
    nch = tm // GLA_CHUNK
    sh, sc, _ = _ada_slices(ada_ref[...], 0)
    h = (_rms(x, g_ref[...]) * (1.0 + sc) + sh).astype(BF16)
    p = _dot(h, wcat_ref[...])
    q = p[:, :kt] * (GLA_DK ** -0.5)
    k = p[:, kt:2 * kt]
    v_ref[...] = p[:, 2 * kt:2 * kt + vt].astype(BF16)
    rs_ref[...] = _silu(p[:, 2 * kt + vt:2 * kt + 2 * vt] + br_ref[...])
    a1 = p[:, 2 * kt + 2 * vt:].astype(BF16)
    logit = _dot(a1, wa2_ref[...]) + ba_ref[...]
    la = _log_sigmoid(logit) * (1.0 / GLA_TAU)
    la_hi = la.astype(BF16)
    la_lo = (la - la_hi.astype(F32)).astype(BF16)

    q3 = q.reshape(nch, GLA_CHUNK, kt)
    k3 = k.reshape(nch, GLA_CHUNK, kt)
    for upper, qd_ref, ki_ref, kd_ref, dl_ref in (
            (False, qdf_ref, kif_ref, kdf_ref, dlf_ref),
            (True, qdb_ref, kib_ref, kdb_ref, dlb_ref)):
        cols = slice(kt, 2 * kt) if upper else slice(0, kt)
        b = _chunk_cumsum(_chunk_tri(upper), la_hi[:, cols], la_lo[:, cols])
        b3 = b.reshape(nch, GLA_CHUNK, kt)
        last = 0 if upper else GLA_CHUNK - 1
        b_last = b3[:, last:last + 1, :]
        qd_ref[...] = (q3 * jnp.exp(b3)).reshape(tm, kt).astype(BF16)
        ki_ref[...] = (k3 * jnp.exp(-b3)).reshape(tm, kt).astype(BF16)
        kd_ref[...] = (k3 * jnp.exp(b_last - b3)).reshape(tm, kt).astype(BF16)
        dl_ref[...] = jnp.exp(jnp.sum(la[:, cols].reshape(nch, GLA_CHUNK, kt), axis=1))


def _gla_proj_call(x, ada3, g, wcat, br, wa2, ba, *, rows_per_cond):
    n, d = x.shape
    kt, vt = GLA_DK_TOT, GLA_DV_TOT
    nch = TOKEN_TILE // GLA_CHUNK
    kspec = _tile_spec(kt)
    dl_spec = pl.BlockSpec((nch, kt), lambda t: (t, 0))
    kshape = jax.ShapeDtypeStruct((n, kt), BF16)
    dl_shape = jax.ShapeDtypeStruct((n // GLA_CHUNK, kt), F32)
    return pl.pallas_call(
        _gla_proj_kernel,
        grid=(n // TOKEN_TILE,),
        in_specs=[
            _tile_spec(d), _ada_spec(rows_per_cond), _resident((1, d)),
            _resident(wcat.shape), _resident(br.shape), _resident(wa2.shape),
            _resident(ba.shape),
        ],
        out_specs=[kspec] * 6 + [_tile_spec(vt), _tile_spec(vt), dl_spec, dl_spec],
        out_shape=[kshape] * 6 + [
            jax.ShapeDtypeStruct((n, vt), BF16), jax.ShapeDtypeStruct((n, vt), F32),
            dl_shape, dl_shape],
        compiler_params=_params(1),
        name="gla_proj",
    )(x, ada3, g, wcat, br, wa2, ba)


def _gla_scan_kernel(*refs, ctx_out):
    lat = refs[0:9]
    ctx = refs[9:18]
    o_l = refs[18]
    o_c = refs[19] if ctx_out else None
    u_scr, p_scr = refs[-2], refs[-1]

    ch = GLA_CHUNK
    r = lax.broadcasted_iota(jnp.int32, (ch, ch), 0)
    c = lax.broadcasted_iota(jnp.int32, (ch, ch), 1)

    steps = []
    for backward in (False, True):
        order = []
        col0 = 0
        for group, o_ref in ((ctx, o_c), (lat, o_l)):
            n = group[6].shape[0] // ch
            idxs = range(n - 1, -1, -1) if backward else range(n)
            order += [(group, o_ref, i, col0 + i) for i in idxs]
            col0 += group[7].shape[0]
        steps.append(order)
    n_steps = len(steps[0])

    def operands(group, backward):
        qdf, kif, kdf, qdb, kib, kdb, v, dlf, dlb = group
        return (qdb, kib, kdb, v, dlb) if backward else (qdf, kif, kdf, v, dlf)

    for backward in (False, True):
        mask = (c >= r) if backward else (c <= r)
        for slot, (group, o_ref, i, _) in enumerate(steps[backward]):
            qd_ref, ki_ref, kd_ref, v_ref, _ = operands(group, backward)
            rows = pl.ds(i * ch, ch)
            v = v_ref[rows, :]
            u_scr[int(backward), slot] = lax.dot_general(
                kd_ref[rows, :], v, (((0,), (0,)), ((), ())), preferred_element_type=F32)
            if o_ref is not None:
                scores = lax.dot_general(qd_ref[rows, :], ki_ref[rows, :],
                                         (((1,), (1,)), ((), ())),
                                         preferred_element_type=F32)
                p_scr[int(backward), slot] = jnp.where(mask, scores, 0.0).astype(BF16)

    def decay_columns(backward):
        tables = [operands(g, backward)[4][...] for g in (ctx, lat)]
        used = sum(t.shape[0] for t in tables)
        tables.append(jnp.zeros((GLA_DK - used, GLA_DK), F32))
        return jnp.transpose(jnp.concatenate(tables, axis=0))

    d_cols = [decay_columns(False), decay_columns(True)]
    state = [jnp.zeros((GLA_DK, GLA_DV), F32), jnp.zeros((GLA_DK, GLA_DV), F32)]
    written = set()
    for slot in range(n_steps):
        for backward in (False, True):
            group, o_ref, i, col = steps[backward][slot]
            qd_ref, _, _, v_ref, _ = operands(group, backward)
            rows = pl.ds(i * ch, ch)
            s = state[backward]
            if o_ref is not None:
                lhs = jnp.concatenate([qd_ref[rows, :], p_scr[int(backward), slot]], axis=1)
                rhs = jnp.concatenate([s.astype(BF16), v_ref[rows, :]], axis=0)
                o = _dot(lhs, rhs)
                key = (id(o_ref), i)
                if key in written:
                    o_ref[rows, :] += o
                else:
                    o_ref[rows, :] = o
                    written.add(key)
            state[backward] = d_cols[backward][:, col:col + 1] * s + u_scr[int(backward), slot]


def _gla_scan_call(lat, ctx, *, batch, ctx_out):
    def specs(group):
        t = group[0].shape[1]
        kspec = pl.BlockSpec((None, t, GLA_DK), lambda b, h: (b, 0, h))
        vspec = pl.BlockSpec((None, t, GLA_DV), lambda b, h: (b, 0, h))
        dspec = pl.BlockSpec((None, group[7].shape[1], GLA_DK), lambda b, h: (b, 0, h))
        return [kspec] * 6 + [vspec, dspec, dspec]

    t_l, t_c = lat[0].shape[1], ctx[0].shape[1]
    n_steps = (t_l + t_c) // GLA_CHUNK
    out_specs = [pl.BlockSpec((None, t_l, GLA_DV), lambda b, h: (b, 0, h))]
    out_shape = [jax.ShapeDtypeStruct((batch, t_l, GLA_DV_TOT), F32)]
    if ctx_out:
        out_specs.append(pl.BlockSpec((None, t_c, GLA_DV), lambda b, h: (b, 0, h)))
        out_shape.append(jax.ShapeDtypeStruct((batch, t_c, GLA_DV_TOT), F32))
    return pl.pallas_call(
        functools.partial(_gla_scan_kernel, ctx_out=ctx_out),
        grid=(batch, GLA_HEADS),
        in_specs=specs(lat) + specs(ctx),
        out_specs=out_specs,
        out_shape=out_shape,
        scratch_shapes=[pltpu.VMEM((2, n_steps, GLA_DK, GLA_DV), F32),
                        pltpu.VMEM((2, n_steps, GLA_CHUNK, GLA_CHUNK), BF16)],
        compiler_params=_params(2),
        name="gla_scan",
    )(*lat, *ctx)


def _gla_out_kernel(x_ref, o_ref, rs_ref, ada_ref, ng_ref, wout_ref, out_ref):
    x = x_ref[...]
    o = o_ref[...]
    _, _, gt = _ada_slices(ada_ref[...], 0)
    ng = ng_ref[...]
    heads = [_rms(o[:, h * GLA_DV:(h + 1) * GLA_DV], ng) for h in range(GLA_HEADS)]
    on = jnp.concatenate(heads, axis=1)
    y = _dot((on * rs_ref[...]).astype(BF16), wout_ref[...])
    out_ref[...] = x + gt * y


def _gla_out_call(x, o, rs, ada3, ng, w_out, *, rows_per_cond):
    n, d = x.shape
    return pl.pallas_call(
        _gla_out_kernel,
        grid=(n // TOKEN_TILE,),
        in_specs=[
            _tile_spec(d), _tile_spec(GLA_DV_TOT), _tile_spec(GLA_DV_TOT),
            _ada_spec(rows_per_cond), _resident(ng.shape), _resident(w_out.shape),
        ],
        out_specs=_tile_spec(d),
        out_shape=jax.ShapeDtypeStruct((n, d), F32),
        compiler_params=_params(1),
        name="gla_out",
    )(x, o, rs, ada3, ng, w_out)


def _gla_weights(w_in, w_a1, w_a2, b_a):
    d = w_in.shape[0]
    pad = jnp.zeros((d, GATE_PAD - 2 * GLA_GATE_RANK), w_in.dtype)
    wcat = jnp.concatenate([w_in, w_a1[0], w_a1[1], pad], axis=1).astype(BF16)
    wa2 = jnp.zeros((GATE_PAD, 2 * GLA_DK_TOT), w_a2.dtype)
    wa2 = wa2.at[:GLA_GATE_RANK, :GLA_DK_TOT].set(w_a2[0])
    wa2 = wa2.at[GLA_GATE_RANK:2 * GLA_GATE_RANK, GLA_DK_TOT:].set(w_a2[1])
    return wcat, wa2.astype(BF16), b_a.reshape(1, 2 * GLA_DK_TOT)


def kernel(x, c, ctx, c_ctx, ada_w, ada_b, norm1_g, norm2_g, conv_w_in, conv_w, conv_w_out,
           gla_w_in, gla_b_r, gla_w_a1, gla_w_a2, gla_b_a, gla_norm_g, gla_w_out,
           ffn_w_in, ffn_w_out, final_g):
    batch, seq, d = x.shape
    ctx_len = ctx.shape[1]
    assert d == D_MODEL and batch == CTX_ADA_ROW < ADA_ROWS
    assert seq % TOKEN_TILE == 0 and (batch * ctx_len) % TOKEN_TILE == 0
    assert TOKEN_TILE % ctx_len == 0 and TOKEN_TILE % GRID_W == 0

    xl = x.reshape(batch * seq, d)
    xc = ctx.reshape(batch * ctx_len, d)
    cond = jnp.concatenate(
        [c, c_ctx[None, :], jnp.zeros((ADA_ROWS - batch - 1, d), c.dtype)], axis=0)
    ada_all = _ada_call(cond, ada_w, ada_b)
    fg = final_g.reshape(1, d)

    for i in range(DEPTH):
        last = i == DEPTH - 1
        kind, j = i % 2, i // 2
        ada3 = ada_all[i].reshape(ADA_ROWS, 1, N_ADA * d)
        g1 = norm1_g[i].reshape(1, d)
        g2 = norm2_g[i].reshape(1, d)
        if kind == 0:
            w_in = conv_w_in[j].astype(BF16)
            w_out = conv_w_out[j].astype(BF16)
            xl = _conv_mix_call(xl, ada3, g1, w_in, conv_w[j], w_out,
                                seg=GRID_W, rows_per_cond=seq)
            if not last:
                xc = _conv_mix_call(xc, ada3, g1, w_in, conv_w[j], w_out,
                                    seg=ctx_len, rows_per_cond=None)
        else:
            wcat, wa2, ba = _gla_weights(gla_w_in[j], gla_w_a1[j], gla_w_a2[j], gla_b_a[j])
            br = gla_b_r[j].reshape(1, GLA_DV_TOT)
            pl_out = _gla_proj_call(xl, ada3, g1, wcat, br, wa2, ba, rows_per_cond=seq)
            pc_out = _gla_proj_call(xc, ada3, g1, wcat, br, wa2, ba, rows_per_cond=None)

            def per_batch(arrs):
                return [a.reshape(batch, -1, a.shape[-1]) for a in arrs]

            def pad_rows(a):
                return jnp.pad(a, ((0, 0), (0, -a.shape[1] % DECAY_ROW_PAD), (0, 0)))

            lat = per_batch(pl_out[:7] + pl_out[8:])
            cx = per_batch(pc_out[:7] + pc_out[8:])
            cx = cx[:7] + [pad_rows(a) for a in cx[7:]]
            outs = _gla_scan_call(lat, cx, batch=batch, ctx_out=not last)
            ng = gla_norm_g[j].reshape(1, GLA_DV)
            w_out = gla_w_out[j].astype(BF16)
            xl = _gla_out_call(xl, outs[0].reshape(batch * seq, GLA_DV_TOT), pl_out[7],
                               ada3, ng, w_out, rows_per_cond=seq)
            if not last:
                xc = _gla_out_call(xc, outs[1].reshape(batch * ctx_len, GLA_DV_TOT),
                                   pc_out[7], ada3, ng, w_out, rows_per_cond=None)
        f_in = ffn_w_in[i].astype(BF16)
        f_out = ffn_w_out[i].astype(BF16)
        xl = _ffn_call(xl, ada3, g2, f_in, f_out, fg, rows_per_cond=seq, final=last)
        if not last:
            xc = _ffn_call(xc, ada3, g2, f_in, f_out, fg, rows_per_cond=None, final=False)
    return xl.reshape(batch, seq, d)
```

```python
import functools
import math

import jax
import jax.numpy as jnp
from jax import lax
from jax.experimental import pallas as pl
from jax.experimental.pallas import tpu as pltpu

F32 = jnp.float32
BF16 = jnp.bfloat16

D_MODEL = 1024
DEPTH = 4
GRID_W = 64
N_ADA = 6
GLA_HEADS = 4
GLA_DK = 128
GLA_DV = 256
GLA_DK_TOT = GLA_HEADS * GLA_DK
GLA_DV_TOT = GLA_HEADS * GLA_DV
GLA_GATE_RANK = 16
GLA_TAU = 16.0
GLA_CHUNK = 64
EPS = 1e-6

TOKEN_TILE = 512
PROJ_TILE = 1024
PROJ_SUB = 256
CUMSUM_ROWS = 256
ADA_ROWS = 24
CTX_ADA_ROW = 16
ADA_COL_TILE = 1536
GATE_PAD = 128
DECAY_ROW_PAD = 8
V7X_VMEM_LIMIT = 56 * 1024 * 1024


def _rms(x, g):
    ms = jnp.mean(x * x, axis=-1, keepdims=True)
    return x * lax.rsqrt(ms + EPS) * g


def _silu(x):
    return x * jax.nn.sigmoid(x)


def _dot(a, b):
    return jnp.dot(a, b, preferred_element_type=F32)


def _resident(shape):
    nd = len(shape)
    return pl.BlockSpec(shape, lambda *_: (0,) * nd, pipeline_mode=pl.Buffered(1))


def _params(n_axes):
    return pltpu.CompilerParams(
        dimension_semantics=("arbitrary",) * n_axes,
        vmem_limit_bytes=V7X_VMEM_LIMIT)


def _ada_kernel(cond_ref, w_ref, b_ref, o_ref):
    s = _silu(cond_ref[...]).astype(BF16)
    o_ref[...] = _dot(s, w_ref[...].astype(BF16)) + b_ref[...]


def _ada_call(cond, ada_w, ada_b):
    depth, d, n = ada_w.shape
    return pl.pallas_call(
        _ada_kernel,
        grid=(depth, n // ADA_COL_TILE),
        in_specs=[
            pl.BlockSpec((ADA_ROWS, d), lambda i, j: (0, 0)),
            pl.BlockSpec((None, d, ADA_COL_TILE), lambda i, j: (i, 0, j)),
            pl.BlockSpec((None, 1, ADA_COL_TILE), lambda i, j: (i, 0, j)),
        ],
        out_specs=pl.BlockSpec((None, ADA_ROWS, ADA_COL_TILE), lambda i, j: (i, 0, j)),
        out_shape=jax.ShapeDtypeStruct((depth, ADA_ROWS, n), F32),
        compiler_params=_params(2),
        name="ada_terms",
    )(cond, ada_w, ada_b.reshape(depth, 1, n))


def _ada_slices(ada, first):
    d = D_MODEL
    return tuple(ada[:, (first + t) * d:(first + t + 1) * d] for t in range(3))


def _ada_spec(rows_per_cond, tile=TOKEN_TILE):
    if rows_per_cond is None:
        return pl.BlockSpec((None, 1, N_ADA * D_MODEL), lambda t: (CTX_ADA_ROW, 0, 0))
    tiles = rows_per_cond // tile
    return pl.BlockSpec((None, 1, N_ADA * D_MODEL), lambda t: (t // tiles, 0, 0))


def _tile_spec(width, tile=TOKEN_TILE):
    return pl.BlockSpec((tile, width), lambda t: (t, 0))


def _conv_mix_kernel(x_ref, ada_ref, g_ref, win_ref, cw_ref, wout_ref, o_ref, *, seg):
    d = D_MODEL
    x = x_ref[...]
    sh, sc, gt = _ada_slices(ada_ref[...], 0)
    h = _rms(x, g_ref[...]) * (1.0 + sc) + sh
    p = _dot(h.astype(BF16), win_ref[...])
    gate_b = p[:, :d]
    u = p[:, d:2 * d] * p[:, 2 * d:]
    tm = x.shape[0]
    pos = lax.broadcasted_iota(jnp.int32, (tm, d), 0) & (seg - 1)
    u_prev = jnp.where(pos == 0, 0.0, pltpu.roll(u, 1, 0))
    u_next = jnp.where(pos == seg - 1, 0.0, pltpu.roll(u, tm - 1, 0))
    cw = cw_ref[...]
    conv = cw[0:1] * u_prev + cw[1:2] * u + cw[2:3] * u_next
    y = _dot((gate_b * conv).astype(BF16), wout_ref[...])
    o_ref[...] = x + gt * y


def _conv_mix_call(x, ada3, g, w_in, cw, w_out, *, seg, rows_per_cond):
    n, d = x.shape
    return pl.pallas_call(
        functools.partial(_conv_mix_kernel, seg=seg),
        grid=(n // TOKEN_TILE,),
        in_specs=[
            _tile_spec(d), _ada_spec(rows_per_cond), _resident((1, d)),
            _resident(w_in.shape), _resident(cw.shape), _resident(w_out.shape),
        ],
        out_specs=_tile_spec(d),
        out_shape=jax.ShapeDtypeStruct((n, d), F32),
        compiler_params=_params(1),
        name="conv_mixer",
    )(x, ada3, g, w_in, cw, w_out)


def _ffn_kernel(*refs, final, gla_out):
    if gla_out:
        (x_ref, ada_ref, g_ref, win_ref, wout_ref, fg_ref,
         o_ref, r_ref, br_ref, ng_ref, wo_ref, out_ref) = refs
    else:
        x_ref, ada_ref, g_ref, win_ref, wout_ref, fg_ref, out_ref = refs
    ada = ada_ref[...]
    x = x_ref[...]
    if gla_out:
        _, _, gt1 = _ada_slices(ada, 0)
        o = o_ref[...]
        ng = ng_ref[...]
        heads = [_rms(o[:, h * GLA_DV:(h + 1) * GLA_DV], ng) for h in range(GLA_HEADS)]
        gated = jnp.concatenate(heads, axis=1) * _silu(r_ref[...] + br_ref[...])
        x = x + gt1 * _dot(gated.astype(BF16), wo_ref[...])
    sh, sc, gt = _ada_slices(ada, 3)
    h = (_rms(x, g_ref[...]) * (1.0 + sc) + sh).astype(BF16)
    gu = _dot(h, win_ref[...])
    f = gu.shape[1] // 2
    a = _silu(gu[:, :f]) * gu[:, f:]
    y = _dot(a.astype(BF16), wout_ref[...])
    out = x + gt * y
    if final:
        out = _rms(out, fg_ref[...])
    out_ref[...] = out


def _ffn_call(x, ada3, g, w_in, w_out, final_g, *, rows_per_cond, final, gla=None):
    n, d = x.shape
    in_specs = [
        _tile_spec(d), _ada_spec(rows_per_cond), _resident((1, d)),
        _resident(w_in.shape), _resident(w_out.shape), _resident((1, d)),
    ]
    args = [x, ada3, g, w_in, w_out, final_g]
    if gla is not None:
        o, r, br, ng, wo = gla
        in_specs += [_tile_spec(GLA_DV_TOT), _tile_spec(GLA_DV_TOT),
                     _resident(br.shape), _resident(ng.shape), _resident(wo.shape)]
        args += [o, r, br, ng, wo]
    return pl.pallas_call(
        functools.partial(_ffn_kernel, final=final, gla_out=gla is not None),
        grid=(n // TOKEN_TILE,),
        in_specs=in_specs,
        out_specs=_tile_spec(d),
        out_shape=jax.ShapeDtypeStruct((n, d), F32),
        compiler_params=_params(1),
        name="gla_out_ffn" if gla is not None else "swiglu_ffn",
    )(*args)


def _chunk_tri():
    r = lax.broadcasted_iota(jnp.int32, (CUMSUM_ROWS, CUMSUM_ROWS), 0)
    c = lax.broadcasted_iota(jnp.int32, (CUMSUM_ROWS, CUMSUM_ROWS), 1)
    same = (r // GLA_CHUNK) == (c // GLA_CHUNK)
    return jnp.stack([same & (c <= r), same & (c >= r)]).astype(BF16)


def _chunk_cumsum(tri_ref, upper, hi, lo):
    parts = []
    for j in range(hi.shape[0] // CUMSUM_ROWS):
        rows = slice(j * CUMSUM_ROWS, (j + 1) * CUMSUM_ROWS)
        parts.append(_dot(tri_ref[int(upper)], hi[rows]) + _dot(tri_ref[int(upper)], lo[rows]))
    return jnp.concatenate(parts, axis=0)


def _log2_sigmoid(x):
    log2e = 1.0 / math.log(2.0)
    return jnp.minimum(x, 0.0) * log2e - jnp.log2(1.0 + jnp.exp2(jnp.abs(x) * -log2e))


def _gla_proj_kernel(x_ref, ada_ref, g_ref, wcat_ref, wa2_ref, ba_ref, tri_ref,
                     qdf_ref, kif_ref, kdf_ref, qdb_ref, kib_ref, kdb_ref,
                     v_ref, r_ref, dlf_ref, dlb_ref):
    kt, vt = GLA_DK_TOT, GLA_DV_TOT
    nch = PROJ_SUB // GLA_CHUNK
    n_sub = x_ref.shape[0] // PROJ_SUB
    sh, sc, _ = _ada_slices(ada_ref[...], 0)

    def project(j):
        rows = pl.ds(j * PROJ_SUB, PROJ_SUB)
        h = (_rms(x_ref[rows, :], g_ref[...]) * (1.0 + sc) + sh).astype(BF16)
        p = _dot(h, wcat_ref[...])
        v_ref[rows, :] = p[:, 2 * kt:2 * kt + vt].astype(BF16)
        r_ref[rows, :] = p[:, 2 * kt + vt:2 * kt + 2 * vt]
        q = p[:, :kt] * (GLA_DK ** -0.5)
        return q, p[:, kt:2 * kt], p[:, 2 * kt + 2 * vt:].astype(BF16)

    def log_decay(a1):
        logit = _dot(a1, wa2_ref[...]) + ba_ref[...]
        la = _log2_sigmoid(logit) * (1.0 / GLA_TAU)
        la_hi = la.astype(BF16)
        return la, la_hi, (la - la_hi.astype(F32)).astype(BF16)

    def decayed(j, q, k, la, la_hi, la_lo):
        rows = pl.ds(j * PROJ_SUB, PROJ_SUB)
        q3 = q.reshape(nch, GLA_CHUNK, kt)
        k3 = k.reshape(nch, GLA_CHUNK, kt)
        for upper, qd_ref, ki_ref, kd_ref, dl_ref in (
                (False, qdf_ref, kif_ref, kdf_ref, dlf_ref),
                (True, qdb_ref, kib_ref, kdb_ref, dlb_ref)):
            cols = slice(kt, 2 * kt) if upper else slice(0, kt)
            b = _chunk_cumsum(tri_ref, upper, la_hi[:, cols], la_lo[:, cols])
            b3 = b.reshape(nch, GLA_CHUNK, kt)
            last = 0 if upper else GLA_CHUNK - 1
            b_last = b3[:, last:last + 1, :]
            qd_ref[rows, :] = (q3 * jnp.exp2(b3)).reshape(PROJ_SUB, kt).astype(BF16)
            ki_ref[rows, :] = (k3 * jnp.exp2(-b3)).reshape(PROJ_SUB, kt).astype(BF16)
            kd_ref[rows, :] = (k3 * jnp.exp2(b_last - b3)).reshape(PROJ_SUB, kt).astype(BF16)
            dl_ref[pl.ds(j * nch, nch), :] = jnp.exp2(
                jnp.sum(la[:, cols].reshape(nch, GLA_CHUNK, kt), axis=1))

    qk, decay = {}, {}
    for j in range(n_sub + 2):
        if j < n_sub:
            q, k, a1 = project(j)
            qk[j] = (q, k)
            decay[j] = a1
        if 1 <= j <= n_sub:
            decay[j - 1] = log_decay(decay[j - 1])
        if j >= 2:
            decayed(j - 2, *qk.pop(j - 2), *decay.pop(j - 2))


def _gla_proj_call(x, ada3, g, wcat, wa2, ba, *, rows_per_cond):
    n, d = x.shape
    kt, vt = GLA_DK_TOT, GLA_DV_TOT
    nch = PROJ_TILE // GLA_CHUNK
    kspec = _tile_spec(kt, PROJ_TILE)
    dl_spec = pl.BlockSpec((nch, kt), lambda t: (t, 0))
    kshape = jax.ShapeDtypeStruct((n, kt), BF16)
    dl_shape = jax.ShapeDtypeStruct((n // GLA_CHUNK, kt), F32)
    tri = _chunk_tri()
    return pl.pallas_call(
        _gla_proj_kernel,
        grid=(n // PROJ_TILE,),
        in_specs=[
            _tile_spec(d, PROJ_TILE), _ada_spec(rows_per_cond, PROJ_TILE), _resident((1, d)),
            _resident(wcat.shape), _resident(wa2.shape), _resident(ba.shape),
            _resident(tri.shape),
        ],
        out_specs=[kspec] * 6 + [_tile_spec(vt, PROJ_TILE), _tile_spec(vt, PROJ_TILE),
                                 dl_spec, dl_spec],
        out_shape=[kshape] * 6 + [
            jax.ShapeDtypeStruct((n, vt), BF16), jax.ShapeDtypeStruct((n, vt), F32),
            dl_shape, dl_shape],
        compiler_params=_params(1),
        name="gla_proj",
    )(x, ada3, g, wcat, wa2, ba, tri)


def _gla_scan_kernel(*refs, ctx_out):
    lat = refs[0:9]
    ctx = refs[9:18]
    o_l = refs[18]
    o_c = refs[19] if ctx_out else None
    u_scr, p_scr = refs[-2], refs[-1]

    ch = GLA_CHUNK
    r = lax.broadcasted_iota(jnp.int32, (ch, ch), 0)
    c = lax.broadcasted_iota(jnp.int32, (ch, ch), 1)

    steps = []
    for backward in (False, True):
        order = []
        col0 = 0
        for group, o_ref in ((ctx, o_c), (lat, o_l)):
            n = group[6].shape[0] // ch
            idxs = range(n - 1, -1, -1) if backward else range(n)
            order += [(group, o_ref, i, col0 + i) for i in idxs]
            col0 += group[7].shape[0]
        steps.append(order)
    n_steps = len(steps[0])

    def operands(group, backward):
        qdf, kif, kdf, qdb, kib, kdb, v, dlf, dlb = group
        return (qdb, kib, kdb, v, dlb) if backward else (qdf, kif, kdf, v, dlf)

    for backward in (False, True):
        mask = (c >= r) if backward else (c <= r)
        for slot, (group, o_ref, i, _) in enumerate(steps[backward]):
            qd_ref, ki_ref, kd_ref, v_ref, _ = operands(group, backward)
            rows = pl.ds(i * ch, ch)
            v = v_ref[rows, :]
            u_scr[int(backward), slot] = lax.dot_general(
                kd_ref[rows, :], v, (((0,), (0,)), ((), ())), preferred_element_type=F32)
            if o_ref is not None:
                scores = lax.dot_general(qd_ref[rows, :], ki_ref[rows, :],
                                         (((1,), (1,)), ((), ())),
                                         preferred_element_type=F32)
                p_scr[int(backward), slot] = jnp.where(mask, scores, 0.0).astype(BF16)

    def decay_columns(backward):
        tables = [operands(g, backward)[4][...] for g in (ctx, lat)]
        used = sum(t.shape[0] for t in tables)
        tables.append(jnp.zeros((GLA_DK - used, GLA_DK), F32))
        return jnp.transpose(jnp.concatenate(tables, axis=0))

    d_cols = [decay_columns(False), decay_columns(True)]
    state = [jnp.zeros((GLA_DK, GLA_DV), F32), jnp.zeros((GLA_DK, GLA_DV), F32)]
    written = set()
    for slot in range(n_steps):
        for backward in (False, True):
            group, o_ref, i, col = steps[backward][slot]
            qd_ref, _, _, v_ref, _ = operands(group, backward)
            rows = pl.ds(i * ch, ch)
            s = state[backward]
            if o_ref is not None:
                lhs = jnp.concatenate([qd_ref[rows, :], p_scr[int(backward), slot]], axis=1)
                rhs = jnp.concatenate([s.astype(BF16), v_ref[rows, :]], axis=0)
                o = _dot(lhs, rhs)
                key = (id(o_ref), i)
                if key in written:
                    o_ref[rows, :] += o
                else:
                    o_ref[rows, :] = o
                    written.add(key)
            state[backward] = d_cols[backward][:, col:col + 1] * s + u_scr[int(backward), slot]


def _gla_scan_call(lat, ctx, *, batch, ctx_out):
    def specs(group):
        t = group[0].shape[1]
        kspec = pl.BlockSpec((None, t, GLA_DK), lambda b, h: (b, 0, h))
        vspec = pl.BlockSpec((None, t, GLA_DV), lambda b, h: (b, 0, h))
        dspec = pl.BlockSpec((None, group[7].shape[1], GLA_DK), lambda b, h: (b, 0, h))
        return [kspec] * 6 + [vspec, dspec, dspec]

    t_l, t_c = lat[0].shape[1], ctx[0].shape[1]
    n_steps = (t_l + t_c) // GLA_CHUNK
    out_specs = [pl.BlockSpec((None, t_l, GLA_DV), lambda b, h: (b, 0, h))]
    out_shape = [jax.ShapeDtypeStruct((batch, t_l, GLA_DV_TOT), F32)]
    if ctx_out:
        out_specs.append(pl.BlockSpec((None, t_c, GLA_DV), lambda b, h: (b, 0, h)))
        out_shape.append(jax.ShapeDtypeStruct((batch, t_c, GLA_DV_TOT), F32))
    return pl.pallas_call(
        functools.partial(_gla_scan_kernel, ctx_out=ctx_out),
        grid=(batch, GLA_HEADS),
        in_specs=specs(lat) + specs(ctx),
        out_specs=out_specs,
        out_shape=out_shape,
        scratch_shapes=[pltpu.VMEM((2, n_steps, GLA_DK, GLA_DV), F32),
                        pltpu.VMEM((2, n_steps, GLA_CHUNK, GLA_CHUNK), BF16)],
        compiler_params=_params(2),
        name="gla_scan",
    )(*lat, *ctx)


def _gla_weights(w_in, w_a1, w_a2, b_a):
    d = w_in.shape[0]
    pad = jnp.zeros((d, GATE_PAD - 2 * GLA_GATE_RANK), w_in.dtype)
    wcat = jnp.concatenate([w_in, w_a1[0], w_a1[1], pad], axis=1).astype(BF16)
    wa2 = jnp.zeros((GATE_PAD, 2 * GLA_DK_TOT), w_a2.dtype)
    wa2 = wa2.at[:GLA_GATE_RANK, :GLA_DK_TOT].set(w_a2[0])
    wa2 = wa2.at[GLA_GATE_RANK:2 * GLA_GATE_RANK, GLA_DK_TOT:].set(w_a2[1])
    return wcat, wa2.astype(BF16), b_a.reshape(1, 2 * GLA_DK_TOT)


def kernel(x, c, ctx, c_ctx, ada_w, ada_b, norm1_g, norm2_g, conv_w_in, conv_w, conv_w_out,
           gla_w_in, gla_b_r, gla_w_a1, gla_w_a2, gla_b_a, gla_norm_g, gla_w_out,
           ffn_w_in, ffn_w_out, final_g):
    batch, seq, d = x.shape
    ctx_len = ctx.shape[1]
    assert d == D_MODEL and batch == CTX_ADA_ROW < ADA_ROWS
    for tile in (TOKEN_TILE, PROJ_TILE):
        assert seq % tile == 0 and (batch * ctx_len) % tile == 0
    assert TOKEN_TILE % ctx_len == 0 and TOKEN_TILE % GRID_W == 0

    xl = x.reshape(batch * seq, d)
    xc = ctx.reshape(batch * ctx_len, d)
    cond = jnp.concatenate(
        [c, c_ctx[None, :], jnp.zeros((ADA_ROWS - batch - 1, d), c.dtype)], axis=0)
    ada_all = _ada_call(cond, ada_w, ada_b)
    fg = final_g.reshape(1, d)

    for i in range(DEPTH):
        last = i == DEPTH - 1
        kind, j = i % 2, i // 2
        ada3 = ada_all[i].reshape(ADA_ROWS, 1, N_ADA * d)
        g1 = norm1_g[i].reshape(1, d)
        g2 = norm2_g[i].reshape(1, d)
        gla_l = gla_c = None
        if kind == 0:
            w_in = conv_w_in[j].astype(BF16)
            w_out = conv_w_out[j].astype(BF16)
            xl = _conv_mix_call(xl, ada3, g1, w_in, conv_w[j], w_out,
                                seg=GRID_W, rows_per_cond=seq)
            if not last:
                xc = _conv_mix_call(xc, ada3, g1, w_in, conv_w[j], w_out,
                                    seg=ctx_len, rows_per_cond=None)
        else:
            wcat, wa2, ba = _gla_weights(gla_w_in[j], gla_w_a1[j], gla_w_a2[j], gla_b_a[j])
            pl_out = _gla_proj_call(xl, ada3, g1, wcat, wa2, ba, rows_per_cond=seq)
            pc_out = _gla_proj_call(xc, ada3, g1, wcat, wa2, ba, rows_per_cond=None)

            def per_batch(arrs):
                return [a.reshape(batch, -1, a.shape[-1]) for a in arrs]

            def pad_rows(a):
                return jnp.pad(a, ((0, 0), (0, -a.shape[1] % DECAY_ROW_PAD), (0, 0)))

            lat = per_batch(pl_out[:7] + pl_out[8:])
            cx = per_batch(pc_out[:7] + pc_out[8:])
            cx = cx[:7] + [pad_rows(a) for a in cx[7:]]
            outs = _gla_scan_call(lat, cx, batch=batch, ctx_out=not last)
            tail = (gla_b_r[j].reshape(1, GLA_DV_TOT), gla_norm_g[j].reshape(1, GLA_DV),
                    gla_w_out[j].astype(BF16))
            gla_l = (outs[0].reshape(batch * seq, GLA_DV_TOT), pl_out[7]) + tail
            if not last:
                gla_c = (outs[1].reshape(batch * ctx_len, GLA_DV_TOT), pc_out[7]) + tail
        f_in = ffn_w_in[i].astype(BF16)
        f_out = ffn_w_out[i].astype(BF16)
        xl = _ffn_call(xl, ada3, g2, f_in, f_out, fg, rows_per_cond=seq, final=last, gla=gla_l)
        if not last:
            xc = _ffn_call(xc, ada3, g2, f_in, f_out, fg, rows_per_cond=None, final=False,
                           gla=gla_c)
    return xl.reshape(batch, seq, d)
```

```python
import functools
import math

import jax
import jax.numpy as jnp
from jax import lax
from jax.experimental import pallas as pl
from jax.experimental.pallas import tpu as pltpu

F32 = jnp.float32
BF16 = jnp.bfloat16

D_MODEL = 1024
DEPTH = 4
GRID_W = 64
N_ADA = 6
GLA_HEADS = 4
GLA_DK = 128
GLA_DV = 256
GLA_DK_TOT = GLA_HEADS * GLA_DK
GLA_DV_TOT = GLA_HEADS * GLA_DV
GLA_GATE_RANK = 16
GLA_TAU = 16.0
GLA_CHUNK = 64
EPS = 1e-6

FFN_TILE = 1024
FFN_TILE_GLA = 512
FFN_SUB = 512
CONV_TILE = 1024
CONV_SUB = 256
PROJ_TILE = 1024
PROJ_SUB = 256
CUMSUM_ROWS = 256
ADA_ROWS = 24
CTX_ADA_ROW = 16
ADA_COL_TILE = 1536
GATE_PAD = 128
DECAY_ROW_PAD = 8
BF16_SUBLANES = 16
V7X_VMEM_LIMIT = 56 * 1024 * 1024


def _rms(x, g):
    ms = jnp.mean(x * x, axis=-1, keepdims=True)
    return x * lax.rsqrt(ms + EPS) * g


def _silu(x):
    return x * jax.nn.sigmoid(x)


def _dot(a, b):
    return jnp.dot(a, b, preferred_element_type=F32)


def _resident(shape):
    nd = len(shape)
    return pl.BlockSpec(shape, lambda *_: (0,) * nd, pipeline_mode=pl.Buffered(1))


def _params(n_axes):
    return pltpu.CompilerParams(
        dimension_semantics=("arbitrary",) * n_axes,
        vmem_limit_bytes=V7X_VMEM_LIMIT)


def _token_call(body, name, tile, in_specs, args, out_specs, out_shape, casts=()):
    steps = args[0].shape[0] // tile
    n_in, n_out, n_cast = len(args), len(out_shape), len(casts)
    in_specs, out_specs, out_shape = list(in_specs), list(out_specs), list(out_shape)
    for w, layer in casts:
        _, r, c = w.shape
        slab = r // steps
        assert slab * steps == r and slab % BF16_SUBLANES == 0
        in_specs.append(pl.BlockSpec((None, slab, c), lambda t, layer=layer: (layer, t, 0)))
        out_specs.append(pl.BlockSpec((slab, c), lambda t: (t, 0)))
        out_shape.append(jax.ShapeDtypeStruct((r, c), BF16))

    def kernel(*refs):
        cast_in = refs[n_in:n_in + n_cast]
        outs = refs[n_in + n_cast:n_in + n_cast + n_out]
        cast_out = refs[n_in + n_cast + n_out:]
        for src, dst in zip(cast_in, cast_out):
            dst[...] = src[...].astype(BF16)
        body(*refs[:n_in], *outs)

    res = pl.pallas_call(
        kernel, grid=(steps,), in_specs=in_specs, out_specs=out_specs, out_shape=out_shape,
        compiler_params=_params(1), name=name,
    )(*args, *[w for w, _ in casts])
    return res[:n_out], res[n_out:]


def _ada_kernel(cond_ref, w_ref, b_ref, o_ref):
    s = _silu(cond_ref[...]).astype(BF16)
    o_ref[...] = _dot(s, w_ref[...].astype(BF16)) + b_ref[...]


def _ada_call(cond, ada_w, ada_b):
    depth, d, n = ada_w.shape
    return pl.pallas_call(
        _ada_kernel,
        grid=(depth, n // ADA_COL_TILE),
        in_specs=[
            pl.BlockSpec((ADA_ROWS, d), lambda i, j: (0, 0)),
            pl.BlockSpec((None, d, ADA_COL_TILE), lambda i, j: (i, 0, j)),
            pl.BlockSpec((None, 1, ADA_COL_TILE), lambda i, j: (i, 0, j)),
        ],
        out_specs=pl.BlockSpec((None, ADA_ROWS, ADA_COL_TILE), lambda i, j: (i, 0, j)),
        out_shape=jax.ShapeDtypeStruct((depth, ADA_ROWS, n), F32),
        compiler_params=_params(2),
        name="ada_terms",
    )(cond, ada_w, ada_b.reshape(depth, 1, n))


def _ada_slices(ada, first):
    d = D_MODEL
    return tuple(ada[:, (first + t) * d:(first + t + 1) * d] for t in range(3))


def _ada_spec(rows_per_cond, tile):
    if rows_per_cond is None:
        return pl.BlockSpec((None, 1, N_ADA * D_MODEL), lambda t: (CTX_ADA_ROW, 0, 0))
    tiles = rows_per_cond // tile
    return pl.BlockSpec((None, 1, N_ADA * D_MODEL), lambda t: (t // tiles, 0, 0))


def _tile_spec(width, tile):
    return pl.BlockSpec((tile, width), lambda t: (t, 0))


def _conv_mix_kernel(x_ref, ada_ref, g_ref, win_ref, cw_ref, wout_ref, o_ref, *, seg):
    d = D_MODEL
    sh, sc, gt = _ada_slices(ada_ref[...], 0)
    cw = cw_ref[...]
    pos = lax.broadcasted_iota(jnp.int32, (CONV_SUB, d), 0) & (seg - 1)

    def project(rows):
        x = x_ref[rows, :]
        h = _rms(x, g_ref[...]) * (1.0 + sc) + sh
        return x, _dot(h.astype(BF16), win_ref[...])

    def mix(rows, x, p):
        gate_b = p[:, :d]
        u = p[:, d:2 * d] * p[:, 2 * d:]
        u_prev = jnp.where(pos == 0, 0.0, pltpu.roll(u, 1, 0))
        u_next = jnp.where(pos == seg - 1, 0.0, pltpu.roll(u, CONV_SUB - 1, 0))
        conv = cw[0:1] * u_prev + cw[1:2] * u + cw[2:3] * u_next
        y = _dot((gate_b * conv).astype(BF16), wout_ref[...])
        o_ref[rows, :] = x + gt * y

    n_sub = x_ref.shape[0] // CONV_SUB
    pending = None
    for j in range(n_sub + 1):
        rows = pl.ds(j * CONV_SUB, CONV_SUB)
        nxt = (rows,) + project(rows) if j < n_sub else None
        if pending is not None:
            mix(*pending)
        pending = nxt


def _conv_mix_call(x, ada3, g, w_in, cw, w_out, *, seg, rows_per_cond, casts=()):
    n, d = x.shape
    assert CONV_SUB % seg == 0
    (out,), cast = _token_call(
        functools.partial(_conv_mix_kernel, seg=seg), "conv_mixer", CONV_TILE,
        [_tile_spec(d, CONV_TILE), _ada_spec(rows_per_cond, CONV_TILE), _resident((1, d)),
         _resident(w_in.shape), _resident(cw.shape), _resident(w_out.shape)],
        [x, ada3, g, w_in, cw, w_out],
        [_tile_spec(d, CONV_TILE)], [jax.ShapeDtypeStruct((n, d), F32)], casts)
    return out, cast


def _ffn_kernel(*refs, final, gla_out):
    if gla_out:
        (x_ref, ada_ref, g_ref, win_ref, wout_ref, fg_ref,
         o_ref, r_ref, br_ref, ng_ref, wo_ref, out_ref) = refs
    else:
        x_ref, ada_ref, g_ref, win_ref, wout_ref, fg_ref, out_ref = refs
    ada = ada_ref[...]
    sh, sc, gt = _ada_slices(ada, 3)

    def expand(rows):
        x = x_ref[rows, :]
        if gla_out:
            _, _, gt1 = _ada_slices(ada, 0)
            o = o_ref[rows, :]
            ng = ng_ref[...]
            heads = [_rms(o[:, h * GLA_DV:(h + 1) * GLA_DV], ng) for h in range(GLA_HEADS)]
            gated = jnp.concatenate(heads, axis=1) * _silu(r_ref[rows, :] + br_ref[...])
            x = x + gt1 * _dot(gated.astype(BF16), wo_ref[...])
        h = (_rms(x, g_ref[...]) * (1.0 + sc) + sh).astype(BF16)
        return x, _dot(h, win_ref[...])

    def contract(rows, x, gu):
        f = gu.shape[1] // 2
        a = _silu(gu[:, :f]) * gu[:, f:]
        out = x + gt * _dot(a.astype(BF16), wout_ref[...])
        if final:
            out = _rms(out, fg_ref[...])
        out_ref[rows, :] = out

    n_sub = x_ref.shape[0] // FFN_SUB
    pending = None
    for j in range(n_sub + 1):
        rows = pl.ds(j * FFN_SUB, FFN_SUB)
        nxt = (rows,) + expand(rows) if j < n_sub else None
        if pending is not None:
            contract(*pending)
        pending = nxt


def _ffn_call(x, ada3, g, w_in, w_out, final_g, *, rows_per_cond, final, gla=None, casts=()):
    n, d = x.shape
    tile = FFN_TILE if gla is None else FFN_TILE_GLA
    in_specs = [
        _tile_spec(d, tile), _ada_spec(rows_per_cond, tile), _resident((1, d)),
        _resident(w_in.shape), _resident(w_out.shape), _resident((1, d)),
    ]
    args = [x, ada3, g, w_in, w_out, final_g]
    if gla is not None:
        o, r, br, ng, wo = gla
        in_specs += [_tile_spec(GLA_DV_TOT, tile), _tile_spec(GLA_DV_TOT, tile),
                     _resident(br.shape), _resident(ng.shape), _resident(wo.shape)]
        args += [o, r, br, ng, wo]
    (out,), cast = _token_call(
        functools.partial(_ffn_kernel, final=final, gla_out=gla is not None),
        "gla_out_ffn" if gla is not None else "swiglu_ffn", tile, in_specs, args,
        [_tile_spec(d, tile)], [jax.ShapeDtypeStruct((n, d), F32)], casts)
    return out, cast


def _chunk_tri():
    r = lax.broadcasted_iota(jnp.int32, (CUMSUM_ROWS, CUMSUM_ROWS), 0)
    c = lax.broadcasted_iota(jnp.int32, (CUMSUM_ROWS, CUMSUM_ROWS), 1)
    same = (r // GLA_CHUNK) == (c // GLA_CHUNK)
    return jnp.stack([same & (c <= r), same & (c >= r)]).astype(BF16)


def _chunk_cumsum(tri_ref, upper, hi, lo):
    parts = []
    for j in range(hi.shape[0] // CUMSUM_ROWS):
        rows = slice(j * CUMSUM_ROWS, (j + 1) * CUMSUM_ROWS)
        parts.append(_dot(tri_ref[int(upper)], hi[rows]) + _dot(tri_ref[int(upper)], lo[rows]))
    return jnp.concatenate(parts, axis=0)


def _log2_sigmoid(x):
    log2e = 1.0 / math.log(2.0)
    return jnp.minimum(x, 0.0) * log2e - jnp.log2(1.0 + jnp.exp2(jnp.abs(x) * -log2e))


def _gla_proj_kernel(x_ref, ada_ref, g_ref, wcat_ref, wa2_ref, ba_ref, tri_ref,
                     qdf_ref, kif_ref, kdf_ref, qdb_ref, kib_ref, kdb_ref,
                     v_ref, r_ref, dlf_ref, dlb_ref):
    kt, vt = GLA_DK_TOT, GLA_DV_TOT
    nch = PROJ_SUB // GLA_CHUNK
    n_sub = x_ref.shape[0] // PROJ_SUB
    sh, sc, _ = _ada_slices(ada_ref[...], 0)

    def project(j):
        rows = pl.ds(j * PROJ_SUB, PROJ_SUB)
        h = (_rms(x_ref[rows, :], g_ref[...]) * (1.0 + sc) + sh).astype(BF16)
        p = _dot(h, wcat_ref[...])
        v_ref[rows, :] = p[:, 2 * kt:2 * kt + vt].astype(BF16)
        r_ref[rows, :] = p[:, 2 * kt + vt:2 * kt + 2 * vt]
        q = p[:, :kt] * (GLA_DK ** -0.5)
        return q, p[:, kt:2 * kt], p[:, 2 * kt + 2 * vt:].astype(BF16)

    def log_decay(a1):
        logit = _dot(a1, wa2_ref[...]) + ba_ref[...]
        la = _log2_sigmoid(logit) * (1.0 / GLA_TAU)
        la_hi = la.astype(BF16)
        return la, la_hi, (la - la_hi.astype(F32)).astype(BF16)

    def decayed(j, q, k, la, la_hi, la_lo):
        rows = pl.ds(j * PROJ_SUB, PROJ_SUB)
        q3 = q.reshape(nch, GLA_CHUNK, kt)
        k3 = k.reshape(nch, GLA_CHUNK, kt)
        for upper, qd_ref, ki_ref, kd_ref, dl_ref in (
                (False, qdf_ref, kif_ref, kdf_ref, dlf_ref),
                (True, qdb_ref, kib_ref, kdb_ref, dlb_ref)):
            cols = slice(kt, 2 * kt) if upper else slice(0, kt)
            b = _chunk_cumsum(tri_ref, upper, la_hi[:, cols], la_lo[:, cols])
            b3 = b.reshape(nch, GLA_CHUNK, kt)
            last = 0 if upper else GLA_CHUNK - 1
            b_last = b3[:, last:last + 1, :]
            qd_ref[rows, :] = (q3 * jnp.exp2(b3)).reshape(PROJ_SUB, kt).astype(BF16)
            ki_ref[rows, :] = (k3 * jnp.exp2(-b3)).reshape(PROJ_SUB, kt).astype(BF16)
            kd_ref[rows, :] = (k3 * jnp.exp2(b_last - b3)).reshape(PROJ_SUB, kt).astype(BF16)
            dl_ref[pl.ds(j * nch, nch), :] = jnp.exp2(
                jnp.sum(la[:, cols].reshape(nch, GLA_CHUNK, kt), axis=1))

    qk, decay = {}, {}
    for j in range(n_sub + 2):
        if j < n_sub:
            q, k, a1 = project(j)
            qk[j] = (q, k)
            decay[j] = a1
        if 1 <= j <= n_sub:
            decay[j - 1] = log_decay(decay[j - 1])
        if j >= 2:
            decayed(j - 2, *qk.pop(j - 2), *decay.pop(j - 2))


def _gla_proj_call(x, ada3, g, wcat, wa2, ba, *, rows_per_cond, casts=()):
    n, d = x.shape
    kt, vt = GLA_DK_TOT, GLA_DV_TOT
    nch = PROJ_TILE // GLA_CHUNK
    kspec = _tile_spec(kt, PROJ_TILE)
    dl_spec = pl.BlockSpec((nch, kt), lambda t: (t, 0))
    kshape = jax.ShapeDtypeStruct((n, kt), BF16)
    dl_shape = jax.ShapeDtypeStruct((n // GLA_CHUNK, kt), F32)
    tri = _chunk_tri()
    return _token_call(
        _gla_proj_kernel, "gla_proj", PROJ_TILE,
        [_tile_spec(d, PROJ_TILE), _ada_spec(rows_per_cond, PROJ_TILE), _resident((1, d)),
         _resident(wcat.shape), _resident(wa2.shape), _resident(ba.shape), _resident(tri.shape)],
        [x, ada3, g, wcat, wa2, ba, tri],
        [kspec] * 6 + [_tile_spec(vt, PROJ_TILE), _tile_spec(vt, PROJ_TILE), dl_spec, dl_spec],
        [kshape] * 6 + [jax.ShapeDtypeStruct((n, vt), BF16), jax.ShapeDtypeStruct((n, vt), F32),
                        dl_shape, dl_shape],
        casts)


def _gla_scan_kernel(*refs, ctx_out):
    lat = refs[0:9]
    ctx = refs[9:18]
    o_l = refs[18]
    o_c = refs[19] if ctx_out else None
    u_scr, p_scr = refs[-2], refs[-1]

    ch = GLA_CHUNK
    r = lax.broadcasted_iota(jnp.int32, (ch, ch), 0)
    c = lax.broadcasted_iota(jnp.int32, (ch, ch), 1)

    steps = []
    for backward in (False, True):
        order = []
        col0 = 0
        for group, o_ref in ((ctx, o_c), (lat, o_l)):
            n = group[6].shape[0] // ch
            idxs = range(n - 1, -1, -1) if backward else range(n)
            order += [(group, o_ref, i, col0 + i) for i in idxs]
            col0 += group[7].shape[0]
        steps.append(order)
    n_steps = len(steps[0])

    def operands(group, backward):
        qdf, kif, kdf, qdb, kib, kdb, v, dlf, dlb = group
        return (qdb, kib, kdb, v, dlb) if backward else (qdf, kif, kdf, v, dlf)

    for backward in (False, True):
        mask = (c >= r) if backward else (c <= r)
        for slot, (group, o_ref, i, _) in enumerate(steps[backward]):
            qd_ref, ki_ref, kd_ref, v_ref, _ = operands(group, backward)
            rows = pl.ds(i * ch, ch)
            v = v_ref[rows, :]
            u_scr[int(backward), slot] = lax.dot_general(
                kd_ref[rows, :], v, (((0,), (0,)), ((), ())), preferred_element_type=F32)
            if o_ref is not None:
                scores = lax.dot_general(qd_ref[rows, :], ki_ref[rows, :],
                                         (((1,), (1,)), ((), ())),
                                         preferred_element_type=F32)
                p_scr[int(backward), slot] = jnp.where(mask, scores, 0.0).astype(BF16)

    def decay_columns(backward):
        tables = [operands(g, backward)[4][...] for g in (ctx, lat)]
        used = sum(t.shape[0] for t in tables)
        tables.append(jnp.zeros((GLA_DK - used, GLA_DK), F32))
        return jnp.transpose(jnp.concatenate(tables, axis=0))

    d_cols = [decay_columns(False), decay_columns(True)]
    state = [jnp.zeros((GLA_DK, GLA_DV), F32), jnp.zeros((GLA_DK, GLA_DV), F32)]
    written = set()
    for slot in range(n_steps):
        for backward in (False, True):
            group, o_ref, i, col = steps[backward][slot]
            qd_ref, _, _, v_ref, _ = operands(group, backward)
            rows = pl.ds(i * ch, ch)
            s = state[backward]
            if o_ref is not None:
                lhs = jnp.concatenate([qd_ref[rows, :], p_scr[int(backward), slot]], axis=1)
                rhs = jnp.concatenate([s.astype(BF16), v_ref[rows, :]], axis=0)
                o = _dot(lhs, rhs)
                key = (id(o_ref), i)
                if key in written:
                    o_ref[rows, :] += o
                else:
                    o_ref[rows, :] = o
                    written.add(key)
            state[backward] = d_cols[backward][:, col:col + 1] * s + u_scr[int(backward), slot]


def _gla_scan_call(lat, ctx, *, batch, ctx_out):
    def specs(group):
        t = group[0].shape[1]
        kspec = pl.BlockSpec((None, t, GLA_DK), lambda b, h: (b, 0, h))
        vspec = pl.BlockSpec((None, t, GLA_DV), lambda b, h: (b, 0, h))
        dspec = pl.BlockSpec((None, group[7].shape[1], GLA_DK), lambda b, h: (b, 0, h))
        return [kspec] * 6 + [vspec, dspec, dspec]

    t_l, t_c = lat[0].shape[1], ctx[0].shape[1]
    n_steps = (t_l + t_c) // GLA_CHUNK
    out_specs = [pl.BlockSpec((None, t_l, GLA_DV), lambda b, h: (b, 0, h))]
    out_shape = [jax.ShapeDtypeStruct((batch, t_l, GLA_DV_TOT), F32)]
    if ctx_out:
        out_specs.append(pl.BlockSpec((None, t_c, GLA_DV), lambda b, h: (b, 0, h)))
        out_shape.append(jax.ShapeDtypeStruct((batch, t_c, GLA_DV_TOT), F32))
    return pl.pallas_call(
        functools.partial(_gla_scan_kernel, ctx_out=ctx_out),
        grid=(batch, GLA_HEADS),
        in_specs=specs(lat) + specs(ctx),
        out_specs=out_specs,
        out_shape=out_shape,
        scratch_shapes=[pltpu.VMEM((2, n_steps, GLA_DK, GLA_DV), F32),
                        pltpu.VMEM((2, n_steps, GLA_CHUNK, GLA_CHUNK), BF16)],
        compiler_params=_params(2),
        name="gla_scan",
    )(*lat, *ctx)


def _gla_weights(w_in, w_a1, w_a2, b_a):
    d = w_in.shape[0]
    pad = jnp.zeros((d, GATE_PAD - 2 * GLA_GATE_RANK), BF16)
    wcat = jnp.concatenate([w_in, w_a1[0].astype(BF16), w_a1[1].astype(BF16), pad], axis=1)
    wa2 = jnp.zeros((GATE_PAD, 2 * GLA_DK_TOT), w_a2.dtype)
    wa2 = wa2.at[:GLA_GATE_RANK, :GLA_DK_TOT].set(w_a2[0])
    wa2 = wa2.at[GLA_GATE_RANK:2 * GLA_GATE_RANK, GLA_DK_TOT:].set(w_a2[1])
    return wcat, wa2.astype(BF16), b_a.reshape(1, 2 * GLA_DK_TOT)


def kernel(x, c, ctx, c_ctx, ada_w, ada_b, norm1_g, norm2_g, conv_w_in, conv_w, conv_w_out,
           gla_w_in, gla_b_r, gla_w_a1, gla_w_a2, gla_b_a, gla_norm_g, gla_w_out,
           ffn_w_in, ffn_w_out, final_g):
    batch, seq, d = x.shape
    ctx_len = ctx.shape[1]
    d_ff = ffn_w_out.shape[1]
    assert d == D_MODEL and batch == CTX_ADA_ROW < ADA_ROWS
    for tile in (CONV_TILE, FFN_TILE, FFN_TILE_GLA, PROJ_TILE):
        assert seq % tile == 0 and (batch * ctx_len) % tile == 0

    xl = x.reshape(batch * seq, d)
    xc = ctx.reshape(batch * ctx_len, d)
    cond = jnp.concatenate(
        [c, c_ctx[None, :], jnp.zeros((ADA_ROWS - batch - 1, d), c.dtype)], axis=0)
    ada_all = _ada_call(cond, ada_w, ada_b)
    fg = final_g.reshape(1, d)
    ffn_w_out_rows = ffn_w_out.reshape(DEPTH, d, d_ff)

    def mixer_casts(i):
        j = i // 2
        if i >= DEPTH:
            return ()
        if i % 2 == 0:
            return ((conv_w_in, j), (conv_w_out, j))
        return ((gla_w_in, j), (gla_w_out, j))

    mix_w = (conv_w_in[0].astype(BF16), conv_w_out[0].astype(BF16))
    for i in range(DEPTH):
        last = i == DEPTH - 1
        kind, j = i % 2, i // 2
        ada3 = ada_all[i].reshape(ADA_ROWS, 1, N_ADA * d)
        g1 = norm1_g[i].reshape(1, d)
        g2 = norm2_g[i].reshape(1, d)
        ffn_casts = ((ffn_w_in, i), (ffn_w_out_rows, i))
        gla_l = gla_c = None
        if kind == 0:
            w_in, w_out = mix_w
            xl, ffn_w = _conv_mix_call(xl, ada3, g1, w_in, conv_w[j], w_out,
                                       seg=GRID_W, rows_per_cond=seq, casts=ffn_casts)
            if not last:
                xc, _ = _conv_mix_call(xc, ada3, g1, w_in, conv_w[j], w_out,
                                       seg=ctx_len, rows_per_cond=None)
        else:
            w_in, w_out = mix_w
            wcat, wa2, ba = _gla_weights(w_in, gla_w_a1[j], gla_w_a2[j], gla_b_a[j])
            pl_out, ffn_w = _gla_proj_call(xl, ada3, g1, wcat, wa2, ba, rows_per_cond=seq,
                                           casts=ffn_casts)
            pc_out, _ = _gla_proj_call(xc, ada3, g1, wcat, wa2, ba, rows_per_cond=None)

            def per_batch(arrs):
                return [a.reshape(batch, -1, a.shape[-1]) for a in arrs]

            def pad_rows(a):
                return jnp.pad(a, ((0, 0), (0, -a.shape[1] % DECAY_ROW_PAD), (0, 0)))

            lat = per_batch(pl_out[:7] + pl_out[8:])
            cx = per_batch(pc_out[:7] + pc_out[8:])
            cx = cx[:7] + [pad_rows(a) for a in cx[7:]]
            outs = _gla_scan_call(lat, cx, batch=batch, ctx_out=not last)
            tail = (gla_b_r[j].reshape(1, GLA_DV_TOT), gla_norm_g[j].reshape(1, GLA_DV), w_out)
            gla_l = (outs[0].reshape(batch * seq, GLA_DV_TOT), pl_out[7]) + tail
            if not last:
                gla_c = (outs[1].reshape(batch * ctx_len, GLA_DV_TOT), pc_out[7]) + tail
        f_in, f_out = ffn_w[0], ffn_w[1].reshape(d_ff, d)
        xl, mix_w = _ffn_call(xl, ada3, g2, f_in, f_out, fg, rows_per_cond=seq, final=last,
                              gla=gla_l, casts=mixer_casts(i + 1))
        if not last:
            xc, _ = _ffn_call(xc, ada3, g2, f_in, f_out, fg, rows_per_cond=None, final=False,
                              gla=gla_c)
    return xl.reshape(batch, seq, d)
```

```python
import functools
import math

import jax
import jax.numpy as jnp
from jax import lax
from jax.experimental import pallas as pl
from jax.experimental.pallas import tpu as pltpu

F32 = jnp.float32
BF16 = jnp.bfloat16

D_MODEL = 1024
DEPTH = 4
GRID_W = 64
N_ADA = 6
GLA_HEADS = 4
GLA_DK = 128
GLA_DV = 256
GLA_DK_TOT = GLA_HEADS * GLA_DK
GLA_DV_TOT = GLA_HEADS * GLA_DV
GLA_GATE_RANK = 16
GLA_TAU = 16.0
GLA_CHUNK = 64
EPS = 1e-6

FFN_TILE = 1024
FFN_TILE_GLA = 512
FFN_SUB = 512
CONV_TILE = 1024
CONV_SUB = 256
PROJ_TILE = 1024
PROJ_SUB = 256
CUMSUM_ROWS = 256
ADA_ROWS = 24
CTX_ADA_ROW = 16
ADA_COL_TILE = 1536
GATE_PAD = 128
DECAY_ROW_PAD = 8
BF16_SUBLANES = 16
V7X_VMEM_LIMIT = 56 * 1024 * 1024


def _rms(x, g):
    ms = jnp.mean(x * x, axis=-1, keepdims=True)
    return x * lax.rsqrt(ms + EPS) * g


def _silu(x):
    return x * jax.nn.sigmoid(x)


def _dot(a, b):
    return jnp.dot(a, b, preferred_element_type=F32)


def _resident(shape):
    nd = len(shape)
    return pl.BlockSpec(shape, lambda *_: (0,) * nd, pipeline_mode=pl.Buffered(1))


def _params(n_axes):
    return pltpu.CompilerParams(
        dimension_semantics=("arbitrary",) * n_axes,
        vmem_limit_bytes=V7X_VMEM_LIMIT)


def _token_call(body, name, tile, in_specs, args, out_specs, out_shape, casts=()):
    steps = args[0].shape[0] // tile
    n_in, n_out = len(args), len(out_shape)
    in_specs, out_specs, out_shape = list(in_specs), list(out_specs), list(out_shape)
    sources = []
    for entry in casts:
        r = entry[0][0].shape[1]
        slab = next(s for s in range(BF16_SUBLANES, r + 1, BF16_SUBLANES)
                    if r % s == 0 and r // s <= steps)
        last = r // slab - 1
        for w, layer in entry:
            assert w.shape[1] == r
            in_specs.append(pl.BlockSpec(
                (None, slab, w.shape[2]),
                lambda t, layer=layer, last=last: (layer, jnp.minimum(t, last), 0)))
            sources.append(w)
        cols = sum(w.shape[2] for w, _ in entry)
        out_specs.append(pl.BlockSpec((slab, cols), lambda t, last=last: (jnp.minimum(t, last), 0)))
        out_shape.append(jax.ShapeDtypeStruct((r, cols), BF16))
    n_src = len(sources)

    def kernel(*refs):
        cast_in = list(refs[n_in:n_in + n_src])
        outs = refs[n_in + n_src:n_in + n_src + n_out]
        cast_out = refs[n_in + n_src + n_out:]
        for entry, dst in zip(casts, cast_out):
            parts = [cast_in.pop(0)[...].astype(BF16) for _ in entry]
            dst[...] = parts[0] if len(parts) == 1 else jnp.concatenate(parts, axis=1)
        body(*refs[:n_in], *outs)

    res = pl.pallas_call(
        kernel, grid=(steps,), in_specs=in_specs, out_specs=out_specs, out_shape=out_shape,
        compiler_params=_params(1), name=name,
    )(*args, *sources)
    return res[:n_out], res[n_out:]


def _ada_kernel(cond_ref, w_ref, b_ref, o_ref):
    s = _silu(cond_ref[...]).astype(BF16)
    o_ref[...] = _dot(s, w_ref[...].astype(BF16)) + b_ref[...]


def _ada_call(cond, ada_w, ada_b):
    depth, d, n = ada_w.shape
    return pl.pallas_call(
        _ada_kernel,
        grid=(depth, n // ADA_COL_TILE),
        in_specs=[
            pl.BlockSpec((ADA_ROWS, d), lambda i, j: (0, 0)),
            pl.BlockSpec((None, d, ADA_COL_TILE), lambda i, j: (i, 0, j)),
            pl.BlockSpec((None, 1, ADA_COL_TILE), lambda i, j: (i, 0, j)),
        ],
        out_specs=pl.BlockSpec((None, ADA_ROWS, ADA_COL_TILE), lambda i, j: (i, 0, j)),
        out_shape=jax.ShapeDtypeStruct((depth, ADA_ROWS, n), F32),
        compiler_params=_params(2),
        name="ada_terms",
    )(cond, ada_w, ada_b.reshape(depth, 1, n))


def _ada_slices(ada, first):
    d = D_MODEL
    return tuple(ada[:, (first + t) * d:(first + t + 1) * d] for t in range(3))


def _ada_spec(rows_per_cond, tile):
    if rows_per_cond is None:
        return pl.BlockSpec((None, 1, N_ADA * D_MODEL), lambda t: (CTX_ADA_ROW, 0, 0))
    tiles = rows_per_cond // tile
    return pl.BlockSpec((None, 1, N_ADA * D_MODEL), lambda t: (t // tiles, 0, 0))


def _tile_spec(width, tile):
    return pl.BlockSpec((tile, width), lambda t: (t, 0))


def _conv_mix_kernel(x_ref, ada_ref, g_ref, win_ref, cw_ref, wout_ref, o_ref, *, seg):
    d = D_MODEL
    sh, sc, gt = _ada_slices(ada_ref[...], 0)
    cw = cw_ref[...]
    pos = lax.broadcasted_iota(jnp.int32, (CONV_SUB, d), 0) & (seg - 1)

    def project(rows):
        x = x_ref[rows, :]
        h = _rms(x, g_ref[...]) * (1.0 + sc) + sh
        return x, _dot(h.astype(BF16), win_ref[...])

    def mix(rows, x, p):
        gate_b = p[:, :d]
        u = p[:, d:2 * d] * p[:, 2 * d:]
        u_prev = jnp.where(pos == 0, 0.0, pltpu.roll(u, 1, 0))
        u_next = jnp.where(pos == seg - 1, 0.0, pltpu.roll(u, CONV_SUB - 1, 0))
        conv = cw[0:1] * u_prev + cw[1:2] * u + cw[2:3] * u_next
        y = _dot((gate_b * conv).astype(BF16), wout_ref[...])
        o_ref[rows, :] = x + gt * y

    n_sub = x_ref.shape[0] // CONV_SUB
    pending = None
    for j in range(n_sub + 1):
        rows = pl.ds(j * CONV_SUB, CONV_SUB)
        nxt = (rows,) + project(rows) if j < n_sub else None
        if pending is not None:
            mix(*pending)
        pending = nxt


def _conv_mix_call(x, ada3, g, w_in, cw, w_out, *, seg, rows_per_cond, casts=()):
    n, d = x.shape
    assert CONV_SUB % seg == 0
    (out,), cast = _token_call(
        functools.partial(_conv_mix_kernel, seg=seg), "conv_mixer", CONV_TILE,
        [_tile_spec(d, CONV_TILE), _ada_spec(rows_per_cond, CONV_TILE), _resident((1, d)),
         _resident(w_in.shape), _resident(cw.shape), _resident(w_out.shape)],
        [x, ada3, g, w_in, cw, w_out],
        [_tile_spec(d, CONV_TILE)], [jax.ShapeDtypeStruct((n, d), F32)], casts)
    return out, cast


def _ffn_kernel(*refs, final, gla_out):
    if gla_out:
        (x_ref, ada_ref, g_ref, win_ref, wout_ref, fg_ref,
         o_ref, r_ref, br_ref, ng_ref, wo_ref, out_ref) = refs
    else:
        x_ref, ada_ref, g_ref, win_ref, wout_ref, fg_ref, out_ref = refs
    ada = ada_ref[...]
    sh, sc, gt = _ada_slices(ada, 3)

    def expand(rows):
        x = x_ref[rows, :]
        if gla_out:
            _, _, gt1 = _ada_slices(ada, 0)
            o = o_ref[rows, :]
            ng = ng_ref[...]
            heads = [_rms(o[:, h * GLA_DV:(h + 1) * GLA_DV], ng) for h in range(GLA_HEADS)]
            gated = jnp.concatenate(heads, axis=1) * _silu(r_ref[rows, :] + br_ref[...])
            x = x + gt1 * _dot(gated.astype(BF16), wo_ref[...])
        h = (_rms(x, g_ref[...]) * (1.0 + sc) + sh).astype(BF16)
        return x, _dot(h, win_ref[...])

    def contract(rows, x, gu):
        f = gu.shape[1] // 2
        a = _silu(gu[:, :f]) * gu[:, f:]
        out = x + gt * _dot(a.astype(BF16), wout_ref[...])
        if final:
            out = _rms(out, fg_ref[...])
        out_ref[rows, :] = out

    n_sub = x_ref.shape[0] // FFN_SUB
    pending = None
    for j in range(n_sub + 1):
        rows = pl.ds(j * FFN_SUB, FFN_SUB)
        nxt = (rows,) + expand(rows) if j < n_sub else None
        if pending is not None:
            contract(*pending)
        pending = nxt


def _ffn_call(x, ada3, g, w_in, w_out, final_g, *, rows_per_cond, final, gla=None, casts=()):
    n, d = x.shape
    tile = FFN_TILE if gla is None else FFN_TILE_GLA
    in_specs = [
        _tile_spec(d, tile), _ada_spec(rows_per_cond, tile), _resident((1, d)),
        _resident(w_in.shape), _resident(w_out.shape), _resident((1, d)),
    ]
    args = [x, ada3, g, w_in, w_out, final_g]
    if gla is not None:
        o, r, br, ng, wo = gla
        in_specs += [_tile_spec(GLA_DV_TOT, tile), _tile_spec(GLA_DV_TOT, tile),
                     _resident(br.shape), _resident(ng.shape), _resident(wo.shape)]
        args += [o, r, br, ng, wo]
    (out,), cast = _token_call(
        functools.partial(_ffn_kernel, final=final, gla_out=gla is not None),
        "gla_out_ffn" if gla is not None else "swiglu_ffn", tile, in_specs, args,
        [_tile_spec(d, tile)], [jax.ShapeDtypeStruct((n, d), F32)], casts)
    return out, cast


def _chunk_tri():
    r = lax.broadcasted_iota(jnp.int32, (CUMSUM_ROWS, CUMSUM_ROWS), 0)
    c = lax.broadcasted_iota(jnp.int32, (CUMSUM_ROWS, CUMSUM_ROWS), 1)
    same = (r // GLA_CHUNK) == (c // GLA_CHUNK)
    return jnp.stack([same & (c <= r), same & (c >= r)]).astype(BF16)


def _chunk_cumsum(tri_ref, upper, hi, lo):
    parts = []
    for j in range(hi.shape[0] // CUMSUM_ROWS):
        rows = slice(j * CUMSUM_ROWS, (j + 1) * CUMSUM_ROWS)
        parts.append(_dot(tri_ref[int(upper)], hi[rows]) + _dot(tri_ref[int(upper)], lo[rows]))
    return jnp.concatenate(parts, axis=0)


def _log2_sigmoid(x):
    log2e = 1.0 / math.log(2.0)
    return jnp.minimum(x, 0.0) * log2e - jnp.log2(1.0 + jnp.exp2(jnp.abs(x) * -log2e))


def _gla_proj_kernel(x_ref, ada_ref, g_ref, wcat_ref, wa2_ref, ba_ref, tri_ref,
                     qdf_ref, kif_ref, kdf_ref, qdb_ref, kib_ref, kdb_ref,
                     v_ref, r_ref, dlf_ref, dlb_ref):
    kt, vt = GLA_DK_TOT, GLA_DV_TOT
    nch = PROJ_SUB // GLA_CHUNK
    n_sub = x_ref.shape[0] // PROJ_SUB
    sh, sc, _ = _ada_slices(ada_ref[...], 0)

    def project(j):
        rows = pl.ds(j * PROJ_SUB, PROJ_SUB)
        h = (_rms(x_ref[rows, :], g_ref[...]) * (1.0 + sc) + sh).astype(BF16)
        p = _dot(h, wcat_ref[...])
        v_ref[rows, :] = p[:, 2 * kt:2 * kt + vt].astype(BF16)
        r_ref[rows, :] = p[:, 2 * kt + vt:2 * kt + 2 * vt]
        q = p[:, :kt] * (GLA_DK ** -0.5)
        return q, p[:, kt:2 * kt], p[:, 2 * kt + 2 * vt:].astype(BF16)

    def log_decay(a1):
        logit = _dot(a1, wa2_ref[...]) + ba_ref[...]
        la = _log2_sigmoid(logit) * (1.0 / GLA_TAU)
        la_hi = la.astype(BF16)
        return la, la_hi, (la - la_hi.astype(F32)).astype(BF16)

    def decayed(j, q, k, la, la_hi, la_lo):
        rows = pl.ds(j * PROJ_SUB, PROJ_SUB)
        q3 = q.reshape(nch, GLA_CHUNK, kt)
        k3 = k.reshape(nch, GLA_CHUNK, kt)
        for upper, qd_ref, ki_ref, kd_ref, dl_ref in (
                (False, qdf_ref, kif_ref, kdf_ref, dlf_ref),
                (True, qdb_ref, kib_ref, kdb_ref, dlb_ref)):
            cols = slice(kt, 2 * kt) if upper else slice(0, kt)
            b = _chunk_cumsum(tri_ref, upper, la_hi[:, cols], la_lo[:, cols])
            b3 = b.reshape(nch, GLA_CHUNK, kt)
            last = 0 if upper else GLA_CHUNK - 1
            b_last = b3[:, last:last + 1, :]
            qd_ref[rows, :] = (q3 * jnp.exp2(b3)).reshape(PROJ_SUB, kt).astype(BF16)
            ki_ref[rows, :] = (k3 * jnp.exp2(-b3)).reshape(PROJ_SUB, kt).astype(BF16)
            kd_ref[rows, :] = (k3 * jnp.exp2(b_last - b3)).reshape(PROJ_SUB, kt).astype(BF16)
            dl_ref[pl.ds(j * nch, nch), :] = jnp.exp2(
                jnp.sum(la[:, cols].reshape(nch, GLA_CHUNK, kt), axis=1))

    qk, decay = {}, {}
    for j in range(n_sub + 2):
        if j < n_sub:
            q, k, a1 = project(j)
            qk[j] = (q, k)
            decay[j] = a1
        if 1 <= j <= n_sub:
            decay[j - 1] = log_decay(decay[j - 1])
        if j >= 2:
            decayed(j - 2, *qk.pop(j - 2), *decay.pop(j - 2))


def _gla_proj_call(x, ada3, g, wcat, wa2, ba, *, rows_per_cond, casts=()):
    n, d = x.shape
    kt, vt = GLA_DK_TOT, GLA_DV_TOT
    nch = PROJ_TILE // GLA_CHUNK
    kspec = _tile_spec(kt, PROJ_TILE)
    dl_spec = pl.BlockSpec((nch, kt), lambda t: (t, 0))
    kshape = jax.ShapeDtypeStruct((n, kt), BF16)
    dl_shape = jax.ShapeDtypeStruct((n // GLA_CHUNK, kt), F32)
    tri = _chunk_tri()
    return _token_call(
        _gla_proj_kernel, "gla_proj", PROJ_TILE,
        [_tile_spec(d, PROJ_TILE), _ada_spec(rows_per_cond, PROJ_TILE), _resident((1, d)),
         _resident(wcat.shape), _resident(wa2.shape), _resident(ba.shape), _resident(tri.shape)],
        [x, ada3, g, wcat, wa2, ba, tri],
        [kspec] * 6 + [_tile_spec(vt, PROJ_TILE), _tile_spec(vt, PROJ_TILE), dl_spec, dl_spec],
        [kshape] * 6 + [jax.ShapeDtypeStruct((n, vt), BF16), jax.ShapeDtypeStruct((n, vt), F32),
                        dl_shape, dl_shape],
        casts)


def _gla_scan_kernel(*refs, ctx_out):
    lat = refs[0:9]
    ctx = refs[9:18]
    o_l = refs[18]
    o_c = refs[19] if ctx_out else None
    u_scr, p_scr = refs[-2], refs[-1]

    ch = GLA_CHUNK
    r = lax.broadcasted_iota(jnp.int32, (ch, ch), 0)
    c = lax.broadcasted_iota(jnp.int32, (ch, ch), 1)

    steps = []
    for backward in (False, True):
        order = []
        col0 = 0
        for group, o_ref in ((ctx, o_c), (lat, o_l)):
            n = group[6].shape[0] // ch
            idxs = range(n - 1, -1, -1) if backward else range(n)
            order += [(group, o_ref, i, col0 + i) for i in idxs]
            col0 += group[7].shape[0]
        steps.append(order)
    n_steps = len(steps[0])

    def operands(group, backward):
        qdf, kif, kdf, qdb, kib, kdb, v, dlf, dlb = group
        return (qdb, kib, kdb, v, dlb) if backward else (qdf, kif, kdf, v, dlf)

    for backward in (False, True):
        mask = (c >= r) if backward else (c <= r)
        for slot, (group, o_ref, i, _) in enumerate(steps[backward]):
            qd_ref, ki_ref, kd_ref, v_ref, _ = operands(group, backward)
            rows = pl.ds(i * ch, ch)
            v = v_ref[rows, :]
            u_scr[int(backward), slot] = lax.dot_general(
                kd_ref[rows, :], v, (((0,), (0,)), ((), ())), preferred_element_type=F32)
            if o_ref is not None:
                scores = lax.dot_general(qd_ref[rows, :], ki_ref[rows, :],
                                         (((1,), (1,)), ((), ())),
                                         preferred_element_type=F32)
                p_scr[int(backward), slot] = jnp.where(mask, scores, 0.0).astype(BF16)

    def decay_columns(backward):
        tables = [operands(g, backward)[4][...] for g in (ctx, lat)]
        used = sum(t.shape[0] for t in tables)
        tables.append(jnp.zeros((GLA_DK - used, GLA_DK), F32))
        return jnp.transpose(jnp.concatenate(tables, axis=0))

    d_cols = [decay_columns(False), decay_columns(True)]
    state = [jnp.zeros((GLA_DK, GLA_DV), F32), jnp.zeros((GLA_DK, GLA_DV), F32)]
    written = set()
    for slot in range(n_steps):
        for backward in (False, True):
            group, o_ref, i, col = steps[backward][slot]
            qd_ref, _, _, v_ref, _ = operands(group, backward)
            rows = pl.ds(i * ch, ch)
            s = state[backward]
            if o_ref is not None:
                lhs = jnp.concatenate([qd_ref[rows, :], p_scr[int(backward), slot]], axis=1)
                rhs = jnp.concatenate([s.astype(BF16), v_ref[rows, :]], axis=0)
                o = _dot(lhs, rhs)
                key = (id(o_ref), i)
                if key in written:
                    o_ref[rows, :] += o
                else:
                    o_ref[rows, :] = o
                    written.add(key)
            state[backward] = d_cols[backward][:, col:col + 1] * s + u_scr[int(backward), slot]


def _gla_scan_call(lat, ctx, *, batch, ctx_out):
    def specs(group):
        t = group[0].shape[1]
        kspec = pl.BlockSpec((None, t, GLA_DK), lambda b, h: (b, 0, h))
        vspec = pl.BlockSpec((None, t, GLA_DV), lambda b, h: (b, 0, h))
        dspec = pl.BlockSpec((None, group[7].shape[1], GLA_DK), lambda b, h: (b, 0, h))
        return [kspec] * 6 + [vspec, dspec, dspec]

    t_l, t_c = lat[0].shape[1], ctx[0].shape[1]
    n_steps = (t_l + t_c) // GLA_CHUNK
    out_specs = [pl.BlockSpec((None, t_l, GLA_DV), lambda b, h: (b, 0, h))]
    out_shape = [jax.ShapeDtypeStruct((batch, t_l, GLA_DV_TOT), F32)]
    if ctx_out:
        out_specs.append(pl.BlockSpec((None, t_c, GLA_DV), lambda b, h: (b, 0, h)))
        out_shape.append(jax.ShapeDtypeStruct((batch, t_c, GLA_DV_TOT), F32))
    return pl.pallas_call(
        functools.partial(_gla_scan_kernel, ctx_out=ctx_out),
        grid=(batch, GLA_HEADS),
        in_specs=specs(lat) + specs(ctx),
        out_specs=out_specs,
        out_shape=out_shape,
        scratch_shapes=[pltpu.VMEM((2, n_steps, GLA_DK, GLA_DV), F32),
                        pltpu.VMEM((2, n_steps, GLA_CHUNK, GLA_CHUNK), BF16)],
        compiler_params=_params(2),
        name="gla_scan",
    )(*lat, *ctx)


def _gate_weights(w_a1, w_a2, b_a):
    n, _, d, rank = w_a1.shape
    a1 = jnp.concatenate(
        [w_a1[:, 0], w_a1[:, 1], jnp.zeros((n, d, GATE_PAD - 2 * rank), w_a1.dtype)], axis=2)
    wa2 = jnp.zeros((n, GATE_PAD, 2 * GLA_DK_TOT), w_a2.dtype)
    wa2 = wa2.at[:, :rank, :GLA_DK_TOT].set(w_a2[:, 0])
    wa2 = wa2.at[:, rank:2 * rank, GLA_DK_TOT:].set(w_a2[:, 1])
    return a1, wa2.astype(BF16), b_a.reshape(n, 1, 2 * GLA_DK_TOT)


def kernel(x, c, ctx, c_ctx, ada_w, ada_b, norm1_g, norm2_g, conv_w_in, conv_w, conv_w_out,
           gla_w_in, gla_b_r, gla_w_a1, gla_w_a2, gla_b_a, gla_norm_g, gla_w_out,
           ffn_w_in, ffn_w_out, final_g):
    batch, seq, d = x.shape
    ctx_len = ctx.shape[1]
    assert d == D_MODEL and batch == CTX_ADA_ROW < ADA_ROWS
    for tile in (CONV_TILE, FFN_TILE, FFN_TILE_GLA, PROJ_TILE):
        assert seq % tile == 0 and (batch * ctx_len) % tile == 0

    xl = x.reshape(batch * seq, d)
    xc = ctx.reshape(batch * ctx_len, d)
    cond = jnp.concatenate(
        [c, c_ctx[None, :], jnp.zeros((ADA_ROWS - batch - 1, d), c.dtype)], axis=0)
    ada_all = _ada_call(cond, ada_w, ada_b)
    fg = final_g.reshape(1, d)
    gate_a1, gate_wa2, gate_ba = _gate_weights(gla_w_a1, gla_w_a2, gla_b_a)

    def mixer_casts(i):
        j = i // 2
        if i >= DEPTH:
            return ()
        if i % 2 == 0:
            return (((conv_w_in, j),), ((conv_w_out, j),))
        return (((gla_w_in, j), (gate_a1, j)), ((gla_w_out, j),))

    mix_w = (conv_w_in[0].astype(BF16), conv_w_out[0].astype(BF16))
    for i in range(DEPTH):
        last = i == DEPTH - 1
        kind, j = i % 2, i // 2
        ada3 = ada_all[i].reshape(ADA_ROWS, 1, N_ADA * d)
        g1 = norm1_g[i].reshape(1, d)
        g2 = norm2_g[i].reshape(1, d)
        ffn_casts = (((ffn_w_in, i),), ((ffn_w_out, i),))
        gla_l = gla_c = None
        if kind == 0:
            w_in, w_out = mix_w
            xl, ffn_w = _conv_mix_call(xl, ada3, g1, w_in, conv_w[j], w_out,
                                       seg=GRID_W, rows_per_cond=seq, casts=ffn_casts)
            if not last:
                xc, _ = _conv_mix_call(xc, ada3, g1, w_in, conv_w[j], w_out,
                                       seg=ctx_len, rows_per_cond=None)
        else:
            wcat, w_out = mix_w
            wa2, ba = gate_wa2[j], gate_ba[j]
            pl_out, ffn_w = _gla_proj_call(xl, ada3, g1, wcat, wa2, ba, rows_per_cond=seq,
                                           casts=ffn_casts)
            pc_out, _ = _gla_proj_call(xc, ada3, g1, wcat, wa2, ba, rows_per_cond=None)

            def per_batch(arrs):
                return [a.reshape(batch, -1, a.shape[-1]) for a in arrs]

            def pad_rows(a):
                return jnp.pad(a, ((0, 0), (0, -a.shape[1] % DECAY_ROW_PAD), (0, 0)))

            lat = per_batch(pl_out[:7] + pl_out[8:])
            cx = per_batch(pc_out[:7] + pc_out[8:])
            cx = cx[:7] + [pad_rows(a) for a in cx[7:]]
            outs = _gla_scan_call(lat, cx, batch=batch, ctx_out=not last)
            tail = (gla_b_r[j].reshape(1, GLA_DV_TOT), gla_norm_g[j].reshape(1, GLA_DV), w_out)
            gla_l = (outs[0].reshape(batch * seq, GLA_DV_TOT), pl_out[7]) + tail
            if not last:
                gla_c = (outs[1].reshape(batch * ctx_len, GLA_DV_TOT), pc_out[7]) + tail
        f_in, f_out = ffn_w
        xl, mix_w = _ffn_call(xl, ada3, g2, f_in, f_out, fg, rows_per_cond=seq, final=last,
                              gla=gla_l, casts=mixer_casts(i + 1))
        if not last:
            xc, _ = _ffn_call(xc, ada3, g2, f_in, f_out, fg, rows_per_cond=None, final=False,
                              gla=gla_c)
    return xl.reshape(batch, seq, d)
```
